```python
import jax, jax.numpy as jnp
from jax import lax

D_MODEL = 1024
BATCH = 4
SEQ = 8192
DEPTH = 2

GRID_W = 64
CTX_LEN = 256
HEAD_DIM = 64
N_NA_HEADS = 8
NA_WIDTH = N_NA_HEADS * HEAD_DIM
N_FOURIER_GROUPS = 4
FOURIER_WIDTH = D_MODEL // 4
FOURIER_GROUP = FOURIER_WIDTH // N_FOURIER_GROUPS
CONV_WIDTH = D_MODEL // 4
CONV_K = 3
WIN_ROWS = 8
WIN_COLS = 16
MLP_HIDDEN = 4 * D_MODEL
N_BRANCHES = 3
N_MOD = 6
SPLIT_SIZES = (FOURIER_WIDTH, CONV_WIDTH, CONV_WIDTH, CONV_WIDTH, NA_WIDTH, NA_WIDTH, NA_WIDTH, D_MODEL, D_MODEL, D_MODEL)
IN_WIDTH = FOURIER_WIDTH + 3 * CONV_WIDTH + 3 * NA_WIDTH + N_BRANCHES * D_MODEL
KV_START = FOURIER_WIDTH + 3 * CONV_WIDTH + NA_WIDTH
KV_END = KV_START + 2 * NA_WIDTH
EPS = 1e-6
NEG = -1e30

kernel_name = "hybrid_fourier_conv_natten_dit_block"


def rmsnorm(x, g):
    xf = x.astype(jnp.float32)
    y = xf * lax.rsqrt(jnp.mean(xf * xf, axis=-1, keepdims=True) + EPS)
    return (y * g.astype(jnp.float32)).astype(x.dtype)


def modulate(h, shift, scale):
    return h * (1 + scale) + shift


def split_proj(p):
    parts, off = [], 0
    for s in SPLIT_SIZES:
        parts.append(p[..., off:off + s])
        off += s
    return parts


def to_heads(t):
    return t.reshape(t.shape[0], t.shape[1], N_NA_HEADS, HEAD_DIM)


def fourier_mix(u):
    b, n, _ = u.shape
    uf = u.astype(jnp.float32).reshape(b, n, N_FOURIER_GROUPS, FOURIER_GROUP)
    y = jnp.fft.fftn(uf, axes=(1, 3), norm="ortho").real
    return y.reshape(b, n, FOURIER_WIDTH).astype(u.dtype)


def short_conv(u, gate_b, gate_c, w):
    z = gate_c * u
    y = lax.conv_general_dilated(
        z, w[:, None, :].astype(z.dtype), window_strides=(1,),
        padding=((CONV_K // 2, CONV_K // 2),),
        dimension_numbers=("NWC", "WIO", "NWC"), feature_group_count=CONV_WIDTH)
    return gate_b * y


def neighbourhood_attention(q, k, v, k_ctx, v_ctx, rel_bias):
    b, n, _ = q.shape
    rows = n // GRID_W
    kr = min(WIN_ROWS, rows)
    kc = WIN_COLS
    grid = (b, rows, GRID_W, N_NA_HEADS, HEAD_DIM)
    q, k, v = q.reshape(grid), k.reshape(grid), v.reshape(grid)
    r = jnp.arange(rows)
    row_start = jnp.clip(r - kr // 2, 0, rows - kr)
    row_idx = row_start[:, None] + jnp.arange(kr)[None, :]
    k_rows = k[:, row_idx]
    v_rows = v[:, row_idx]
    cols = jnp.arange(GRID_W)
    col_start = jnp.clip(cols - kc // 2, 0, GRID_W - kc)
    col_in = (cols[None, :] >= col_start[:, None]) & (cols[None, :] < col_start[:, None] + kc)
    dr = row_idx - r[:, None] + (WIN_ROWS - 1)
    dc = jnp.clip(cols[None, :] - cols[:, None], -(WIN_COLS - 1), WIN_COLS - 1) + (WIN_COLS - 1)
    bias = rel_bias[:, dr[:, None, :, None], dc[None, :, None, :]].astype(jnp.float32)
    scale = HEAD_DIM ** -0.5
    s_win = jnp.einsum('brqhd,brikhd->bhrqik', q, k_rows).astype(jnp.float32) * scale + bias
    s_win = jnp.where(col_in[None, None, None, :, None, :], s_win, NEG)
    s_ctx = jnp.einsum('brqhd,bjhd->bhrqj', q, k_ctx).astype(jnp.float32) * scale
    n_win = kr * GRID_W
    logits = jnp.concatenate([s_win.reshape(b, N_NA_HEADS, rows, GRID_W, n_win), s_ctx], axis=-1)
    p = jax.nn.softmax(logits, axis=-1).astype(v.dtype)
    p_win = p[..., :n_win].reshape(s_win.shape)
    p_ctx = p[..., n_win:]
    out = (jnp.einsum('bhrqik,brikhd->brqhd', p_win, v_rows)
           + jnp.einsum('bhrqj,bjhd->brqhd', p_ctx, v_ctx))
    return out.reshape(b, n, NA_WIDTH)


def context_attention(q, k, v):
    b, l = q.shape[0], q.shape[1]
    s = jnp.einsum('blhd,bmhd->bhlm', q, k).astype(jnp.float32) * (HEAD_DIM ** -0.5)
    p = jax.nn.softmax(s, axis=-1).astype(v.dtype)
    return jnp.einsum('bhlm,bmhd->blhd', p, v).reshape(b, l, NA_WIDTH)


def merge_branches(f, cv, at, g_f, g_c, g_a, w_f, w_c, w_a, w_o):
    m = (jax.nn.sigmoid(g_f) * (f @ w_f) + jax.nn.sigmoid(g_c) * (cv @ w_c)
         + jax.nn.sigmoid(g_a) * (at @ w_a))
    return m @ w_o


def sq_relu_mlp(h, w1, w2):
    a = jax.nn.relu(h @ w1)
    return (a * a) @ w2


def setup_inputs(seed: int = 0) -> dict:
    key = jax.random.key(seed)
    ks = jax.random.split(key, 20)
    nrm = lambda k, shape, s: jax.random.normal(k, shape, jnp.float32) * s
    return {
        "x": nrm(ks[0], (BATCH, SEQ, D_MODEL), 1.0),
        "c": nrm(ks[1], (BATCH, D_MODEL), 1.0),
        "ctx": nrm(ks[2], (BATCH, CTX_LEN, D_MODEL), 1.0),
        "c_ctx": nrm(ks[3], (D_MODEL,), 1.0),
        "ada_w": nrm(ks[4], (DEPTH, D_MODEL, N_MOD * D_MODEL), 0.5 * D_MODEL ** -0.5),
        "ada_b": nrm(ks[5], (DEPTH, N_MOD * D_MODEL), 0.02),
        "norm1_g": 1.0 + nrm(ks[6], (DEPTH, D_MODEL), 0.02),
        "norm2_g": 1.0 + nrm(ks[7], (DEPTH, D_MODEL), 0.02),
        "w_in": nrm(ks[8], (DEPTH, D_MODEL, IN_WIDTH), D_MODEL ** -0.5),
        "conv_w": nrm(ks[9], (DEPTH, CONV_K, CONV_WIDTH), CONV_K ** -0.5),
        "rel_bias": nrm(ks[10], (DEPTH, N_NA_HEADS, 2 * WIN_ROWS - 1, 2 * WIN_COLS - 1), 0.1),
        "w_fourier": nrm(ks[11], (DEPTH, FOURIER_WIDTH, D_MODEL), FOURIER_WIDTH ** -0.5),
        "w_conv": nrm(ks[12], (DEPTH, CONV_WIDTH, D_MODEL), CONV_WIDTH ** -0.5),
        "w_attn": nrm(ks[13], (DEPTH, NA_WIDTH, D_MODEL), NA_WIDTH ** -0.5),
        "w_o": nrm(ks[14], (DEPTH, D_MODEL, D_MODEL), D_MODEL ** -0.5),
        "mlp_w1": nrm(ks[15], (DEPTH, D_MODEL, MLP_HIDDEN), D_MODEL ** -0.5),
        "mlp_w2": nrm(ks[16], (DEPTH, MLP_HIDDEN, D_MODEL), MLP_HIDDEN ** -0.5),
        "final_g": 1.0 + nrm(ks[17], (D_MODEL,), 0.02),
    }


def reference(x, c, ctx, c_ctx, ada_w, ada_b, norm1_g, norm2_g, w_in, conv_w, rel_bias,
              w_fourier, w_conv, w_attn, w_o, mlp_w1, mlp_w2, final_g):
    for l in range(DEPTH):
        last = l == DEPTH - 1
        mod = jax.nn.silu(c) @ ada_w[l] + ada_b[l]
        sh1, sc1, g1, sh2, sc2, g2 = [m[:, None, :] for m in jnp.split(mod, N_MOD, axis=-1)]
        mod_c = jax.nn.silu(c_ctx) @ ada_w[l] + ada_b[l]
        csh1, csc1, cg1, csh2, csc2, cg2 = jnp.split(mod_c, N_MOD, axis=-1)

        hc = modulate(rmsnorm(ctx, norm1_g[l]), csh1, csc1)
        if last:
            kv_c = hc @ w_in[l][:, KV_START:KV_END]
            k_c, v_c = to_heads(kv_c[..., :NA_WIDTH]), to_heads(kv_c[..., NA_WIDTH:])
        else:
            pc = split_proj(hc @ w_in[l])
            k_c, v_c = to_heads(pc[5]), to_heads(pc[6])

        hx = modulate(rmsnorm(x, norm1_g[l]), sh1, sc1)
        px = split_proj(hx @ w_in[l])
        f_x = fourier_mix(px[0])
        cv_x = short_conv(px[1], px[2], px[3], conv_w[l])
        at_x = neighbourhood_attention(px[4], px[5], px[6], k_c, v_c, rel_bias[l])
        x = x + g1 * merge_branches(f_x, cv_x, at_x, px[7], px[8], px[9],
                                    w_fourier[l], w_conv[l], w_attn[l], w_o[l])

        if not last:
            f_c = fourier_mix(pc[0])
            cv_c = short_conv(pc[1], pc[2], pc[3], conv_w[l])
            at_c = context_attention(to_heads(pc[4]), k_c, v_c)
            ctx = ctx + cg1 * merge_branches(f_c, cv_c, at_c, pc[7], pc[8], pc[9],
                                             w_fourier[l], w_conv[l], w_attn[l], w_o[l])

        x = x + g2 * sq_relu_mlp(modulate(rmsnorm(x, norm2_g[l]), sh2, sc2), mlp_w1[l], mlp_w2[l])
        if not last:
            ctx = ctx + cg2 * sq_relu_mlp(modulate(rmsnorm(ctx, norm2_g[l]), csh2, csc2),
                                          mlp_w1[l], mlp_w2[l])
    return rmsnorm(x, final_g)
```

```python
import functools
import math

import numpy as np
import jax
import jax.numpy as jnp
from jax import lax
from jax.experimental import pallas as pl
from jax.experimental.pallas import tpu as pltpu

D_MODEL = 1024
DEPTH = 2
GRID_W = 64
HEAD_DIM = 64
N_NA_HEADS = 8
NA_WIDTH = N_NA_HEADS * HEAD_DIM
N_FOURIER_GROUPS = 4
FOURIER_WIDTH = D_MODEL // 4
FOURIER_GROUP = FOURIER_WIDTH // N_FOURIER_GROUPS
CONV_WIDTH = D_MODEL // 4
CONV_K = 3
WIN_ROWS = 8
WIN_COLS = 16
MLP_HIDDEN = 4 * D_MODEL
N_MOD = 6
EPS = 1e-6
NEG = -1e30

SEG_FOURIER = (0, FOURIER_WIDTH)
SEG_CONV = (SEG_FOURIER[1], SEG_FOURIER[1] + 3 * CONV_WIDTH)
SEG_QKV = (SEG_CONV[1], SEG_CONV[1] + 3 * NA_WIDTH)
SEG_GATES = (SEG_QKV[1], SEG_QKV[1] + 3 * D_MODEL)
KV_START = SEG_QKV[0] + NA_WIDTH
KV_END = SEG_QKV[1]

LANES = 128
MXU_DIM = 256
VMEM_LIMIT_BYTES = 56 * 1024 * 1024

TM_LATENT = 512
ROWS_PER_ATTN_STEP = 4
ATTN_Q = ROWS_PER_ATTN_STEP * GRID_W
ATTN_KEY_BLOCKS = 3
DFT_N1 = 64
DFT_LANE_TILE = 4096
DFT_K1_PER_STEP = 8
MOD_COL_TILE = 1536
PROJ_COL_CHUNK = 512
MLP_HIDDEN_CHUNK = 1024

BF16 = jnp.bfloat16
F32 = jnp.float32


def _params(*semantics):
    return pltpu.CompilerParams(dimension_semantics=semantics, vmem_limit_bytes=VMEM_LIMIT_BYTES)


def _dot(a, b):
    return jnp.dot(a, b, preferred_element_type=F32)


def _dot_nt(a, b):
    return lax.dot_general(a, b, (((1,), (1,)), ((), ())), preferred_element_type=F32)


def _channel_dft_tables():
    c = np.arange(FOURIER_GROUP)
    ang = 2.0 * np.pi * ((c[:, None] * c[None, :]) % FOURIER_GROUP) / FOURIER_GROUP
    eye = np.eye(N_FOURIER_GROUPS)
    s = 1.0 / math.sqrt(FOURIER_GROUP)
    return np.stack([np.kron(eye, np.cos(ang) * s), np.kron(eye, np.sin(ang) * s)]).astype(np.float32)


def _latent_dft_tables(n):
    n1, n2 = DFT_N1, n // DFT_N1
    a = np.arange(n1)
    ang1 = 2.0 * np.pi * ((a[:, None] * a[None, :]) % n1) / n1
    s1 = 1.0 / math.sqrt(n1)
    f1 = np.concatenate([np.cos(ang1) * s1, -np.sin(ang1) * s1], axis=0)
    k1 = np.arange(n1)[:, None, None]
    k2 = np.arange(n2)[None, :, None]
    m2 = np.arange(n2)[None, None, :]
    ang = 2.0 * np.pi * ((m2 * (k1 + n1 * k2)) % n) / n
    s2 = 1.0 / math.sqrt(n2)
    tr, ti = np.cos(ang) * s2, -np.sin(ang) * s2
    g = np.concatenate([np.concatenate([tr, -ti], axis=2),
                        np.concatenate([ti, tr], axis=2)], axis=1)
    return f1.astype(np.float32), g.astype(np.float32)


def _context_dft_table(n):
    a = np.arange(n)
    ang = 2.0 * np.pi * ((a[:, None] * a[None, :]) % n) / n
    s = 1.0 / math.sqrt(n)
    return np.concatenate([np.cos(ang) * s, -np.sin(ang) * s], axis=0).astype(np.float32)


def _attn_key_block_start(rb, n_row_blocks):
    return jnp.clip(rb - 1, 0, n_row_blocks - ATTN_KEY_BLOCKS)


def _attn_bias_table(rel_bias_l, rows):
    h, ndr, _ = rel_bias_l.shape
    w = GRID_W
    ends = w - WIN_COLS
    ext = jnp.concatenate([
        jnp.broadcast_to(rel_bias_l[:, :, :1], (h, ndr, ends)), rel_bias_l,
        jnp.broadcast_to(rel_bias_l[:, :, -1:], (h, ndr, ends)), jnp.zeros((h, ndr, 1), F32)], axis=-1)
    skew = jnp.tile(ext, (1, 1, w))[:, :, :w * (2 * w - 1)].reshape(h, ndr, w, 2 * w - 1)
    slabs = skew[:, :, :, w - 1:]
    qc = np.arange(w)
    cs = np.clip(qc - WIN_COLS // 2, 0, w - WIN_COLS)[:, None]
    col_ok = (qc[None, :] >= cs) & (qc[None, :] < cs + WIN_COLS)
    slabs = jnp.where(col_ok[None, None], slabs, NEG)
    masked = jnp.full((h, w, w), NEG, F32)

    rq, nkr = ROWS_PER_ATTN_STEP, ROWS_PER_ATTN_STEP * ATTN_KEY_BLOCKS
    variants = [(0, 0), (rq, 0), (rows - rq, rows - nkr)]
    tables = []
    for r0, ks in variants:
        q_rows = []
        for qi in range(rq):
            qr = r0 + qi
            rs = min(max(qr - WIN_ROWS // 2, 0), rows - WIN_ROWS)
            tiles = []
            for kj in range(nkr):
                kr = ks + kj
                tiles.append(slabs[:, kr - qr + WIN_ROWS - 1] if rs <= kr < rs + WIN_ROWS else masked)
            q_rows.append(jnp.concatenate(tiles, axis=-1))
        tables.append(jnp.concatenate(q_rows, axis=-2))
    return jnp.stack(tables)


def _mod_kernel(c_ref, w_ref, b_ref, o_ref):
    c = c_ref[...]
    s = c * (1.0 / (1.0 + jnp.exp(-c)))
    o_ref[...] = jnp.dot(s, w_ref[...], precision=lax.Precision.HIGHEST,
                         preferred_element_type=F32) + b_ref[...]


def _modulation(cond, ada_w, ada_b):
    r, d = cond.shape
    depth, _, n = ada_w.shape
    tn = MOD_COL_TILE
    return pl.pallas_call(
        _mod_kernel,
        grid=(depth, n // tn),
        in_specs=[
            pl.BlockSpec((r, d), lambda l, j: (0, 0)),
            pl.BlockSpec((None, d, tn), lambda l, j: (l, 0, j)),
            pl.BlockSpec((None, 1, tn), lambda l, j: (l, 0, j)),
        ],
        out_specs=pl.BlockSpec((None, r, tn), lambda l, j: (l, 0, j)),
        out_shape=jax.ShapeDtypeStruct((depth, r, n), F32),
        compiler_params=_params("parallel", "parallel"),
        name="adaln_modulation",
    )(cond, ada_w, ada_b.reshape(depth, 1, n))


def _norm_modulate(x, g, shift, scale):
    ms = jnp.mean(x * x, axis=-1, keepdims=True)
    y = x * lax.rsqrt(ms + EPS) * g
    return y * (1.0 + scale) + shift


def _sigmoid(x):
    return 1.0 / (1.0 + jnp.exp(-x))


def _inproj_kernel(x_ref, mod_ref, g_ref, w_ref, *out_refs, segments, shift_row, scale_row):
    h = _norm_modulate(x_ref[...], g_ref[...],
                       mod_ref[0, shift_row:shift_row + 1, :], mod_ref[0, scale_row:scale_row + 1, :])
    hb = h.astype(BF16)
    for o_ref, (lo, hi) in zip(out_refs, segments):
        for c0 in range(lo, hi, PROJ_COL_CHUNK):
            c1 = min(c0 + PROJ_COL_CHUNK, hi)
            o_ref[:, c0 - lo:c1 - lo] = _dot(hb, w_ref[:, c0:c1]).astype(o_ref.dtype)


def _in_projection(x2d, mod, g, w_bf16, segments, seq, tm, shift_row, scale_row):
    m, d = x2d.shape
    n = w_bf16.shape[1]
    tiles_per_seq = seq // tm
    kern = functools.partial(_inproj_kernel, segments=tuple(segments),
                             shift_row=shift_row, scale_row=scale_row)
    return pl.pallas_call(
        kern,
        grid=(m // tm,),
        in_specs=[
            pl.BlockSpec((tm, d), lambda i: (i, 0)),
            pl.BlockSpec((1, N_MOD, d), lambda i: (i // tiles_per_seq, 0, 0)),
            pl.BlockSpec((1, d), lambda i: (0, 0)),
            pl.BlockSpec((d, n), lambda i: (0, 0), pipeline_mode=pl.Buffered(1)),
        ],
        out_specs=[pl.BlockSpec((tm, hi - lo), lambda i: (i, 0)) for lo, hi in segments],
        out_shape=[jax.ShapeDtypeStruct((m, hi - lo), BF16) for lo, hi in segments],
        compiler_params=_params("parallel"),
        name="norm_inproj",
    )(x2d, mod, g, w_bf16)


def _dft_stage1_kernel(x_ref, f_ref, o_ref):
    o_ref[...] = _dot(f_ref[...].astype(BF16), x_ref[...]).astype(o_ref.dtype)


def _dft_stage2_kernel(z_ref, g_ref, cs_ref, o_ref, *, k1_per_step, n2, c):
    cos_t, sin_t = cs_ref[0].astype(BF16), cs_ref[1].astype(BF16)
    for j in range(k1_per_step):
        zz = z_ref[:, j].reshape(2 * n2, c)
        p = _dot(g_ref[j].astype(BF16), zz).astype(BF16)
        y = _dot(p[:n2], cos_t) + _dot(p[n2:], sin_t)
        o_ref[:, j * c:(j + 1) * c] = y.astype(o_ref.dtype)


def _fourier_latent(u, batch, seq):
    c = u.shape[1]
    n1, n2 = DFT_N1, seq // DFT_N1
    f1_np, g_np = _latent_dft_tables(seq)
    f1 = jnp.asarray(f1_np)
    g = jnp.asarray(g_np)
    cs = jnp.asarray(_channel_dft_tables())
    lt = min(DFT_LANE_TILE, n2 * c)
    z = pl.pallas_call(
        _dft_stage1_kernel,
        grid=(batch, (n2 * c) // lt),
        in_specs=[
            pl.BlockSpec((None, n1, lt), lambda b, j: (b, 0, j)),
            pl.BlockSpec((2 * n1, n1), lambda b, j: (0, 0)),
        ],
        out_specs=pl.BlockSpec((None, 2 * n1, lt), lambda b, j: (b, 0, j)),
        out_shape=jax.ShapeDtypeStruct((batch, 2 * n1, n2 * c), BF16),
        compiler_params=_params("parallel", "parallel"),
        name="dft_stage1",
    )(u.reshape(batch, n1, n2 * c), f1)
    kb = DFT_K1_PER_STEP
    kern = functools.partial(_dft_stage2_kernel, k1_per_step=kb, n2=n2, c=c)
    y = pl.pallas_call(
        kern,
        grid=(n1 // kb, batch),
        in_specs=[
            pl.BlockSpec((None, 2, kb, n2, c), lambda k, b: (b, 0, k, 0, 0)),
            pl.BlockSpec((kb, 2 * n2, 2 * n2), lambda k, b: (k, 0, 0)),
            pl.BlockSpec((2, c, c), lambda k, b: (0, 0, 0)),
        ],
        out_specs=pl.BlockSpec((None, n2, kb * c), lambda k, b: (b, 0, k)),
        out_shape=jax.ShapeDtypeStruct((batch, n2, n1 * c), BF16),
        compiler_params=_params("parallel", "parallel"),
        name="dft_stage2",
    )(z.reshape(batch, 2, n1, n2, c), g, cs)
    return y.reshape(batch * seq, c)


def _dft_context_kernel(u_ref, f_ref, cs_ref, o_ref, *, n):
    p = _dot(f_ref[...].astype(BF16), u_ref[...]).astype(BF16)
    y = _dot(p[:n], cs_ref[0].astype(BF16)) + _dot(p[n:], cs_ref[1].astype(BF16))
    o_ref[...] = y.astype(o_ref.dtype)


def _fourier_context(u, batch, seq):
    c = u.shape[1]
    f = jnp.asarray(_context_dft_table(seq))
    cs = jnp.asarray(_channel_dft_tables())
    return pl.pallas_call(
        functools.partial(_dft_context_kernel, n=seq),
        grid=(batch,),
        in_specs=[
            pl.BlockSpec((seq, c), lambda b: (b, 0)),
            pl.BlockSpec((2 * seq, seq), lambda b: (0, 0)),
            pl.BlockSpec((2, c, c), lambda b: (0, 0, 0)),
        ],
        out_specs=pl.BlockSpec((seq, c), lambda b: (b, 0)),
        out_shape=jax.ShapeDtypeStruct((batch * seq, c), BF16),
        compiler_params=_params("parallel"),
        name="dft_context",
    )(u, f, cs)


def _softmax_pv(q_pair, keys, values, biases):
    lane = lax.broadcasted_iota(jnp.int32, q_pair.shape, 1)
    low = lane < HEAD_DIM
    zero = jnp.zeros_like(q_pair)
    scale = HEAD_DIM ** -0.5
    outs = []
    for half in range(2):
        qm = jnp.where(low if half == 0 else jnp.logical_not(low), q_pair, zero)
        s = []
        for kj, bj in zip(keys, biases):
            sj = _dot_nt(qm, kj) * scale
            if bj is not None:
                sj = sj + bj[half]
            s.append(sj)
        m = s[0]
        for sj in s[1:]:
            m = jnp.maximum(m, sj)
        m = jnp.max(m, axis=-1, keepdims=True)
        e = [jnp.exp(sj - m) for sj in s]
        tot = e[0]
        for ej in e[1:]:
            tot = tot + ej
        denom = jnp.sum(tot, axis=-1, keepdims=True)
        acc = _dot(e[0].astype(BF16), values[0])
        for ej, vj in zip(e[1:], values[1:]):
            acc = acc + _dot(ej.astype(BF16), vj)
        outs.append(acc * (1.0 / denom))
    out_low = lax.broadcasted_iota(jnp.int32, outs[0].shape, 1) < HEAD_DIM
    return jnp.where(out_low, outs[0], outs[1])


def _natten_kernel(q_ref, *refs):
    nb = ATTN_KEY_BLOCKS
    k_refs, v_refs = refs[:nb], refs[nb:2 * nb]
    kc_ref, vc_ref, bias_ref, o_ref = refs[2 * nb:]
    kw = ATTN_Q
    for p in range(N_NA_HEADS // 2):
        sl = slice(p * LANES, (p + 1) * LANES)
        keys = [r[:, sl] for r in k_refs] + [kc_ref[:, sl]]
        values = [r[:, sl] for r in v_refs] + [vc_ref[:, sl]]
        biases = [bias_ref[2 * p:2 * p + 2, :, j * kw:(j + 1) * kw] for j in range(nb)] + [None]
        o_ref[:, sl] = _softmax_pv(q_ref[:, sl], keys, values, biases).astype(o_ref.dtype)


def _neighbourhood_attention(qkv, kv_ctx, ctx_k_col, bias_table, batch, seq, ctx_len):
    m = qkv.shape[0]
    tq = ATTN_Q
    nrb = seq // tq
    assert ctx_len == tq, "context keys are processed as one key block"

    def q_map(b, r):
        return (b * nrb + r, 0)

    def kv_map(j, col):
        return lambda b, r: (b * nrb + _attn_key_block_start(r, nrb) + j, col)

    def bias_map(b, r):
        return (jnp.where(r == 0, 0, jnp.where(r == nrb - 1, 2, 1)), 0, 0, 0)

    blk = (tq, NA_WIDTH)
    in_specs = [pl.BlockSpec(blk, q_map)]
    in_specs += [pl.BlockSpec(blk, kv_map(j, 1)) for j in range(ATTN_KEY_BLOCKS)]
    in_specs += [pl.BlockSpec(blk, kv_map(j, 2)) for j in range(ATTN_KEY_BLOCKS)]
    in_specs += [
        pl.BlockSpec((ctx_len, NA_WIDTH), lambda b, r: (b, ctx_k_col)),
        pl.BlockSpec((ctx_len, NA_WIDTH), lambda b, r: (b, ctx_k_col + 1)),
        pl.BlockSpec((None, N_NA_HEADS, tq, ATTN_KEY_BLOCKS * tq), bias_map),
    ]
    args = [qkv] * (1 + 2 * ATTN_KEY_BLOCKS) + [kv_ctx, kv_ctx, bias_table]
    return pl.pallas_call(
        _natten_kernel,
        grid=(batch, nrb),
        in_specs=in_specs,
        out_specs=pl.BlockSpec(blk, q_map),
        out_shape=jax.ShapeDtypeStruct((m, NA_WIDTH), BF16),
        compiler_params=_params("parallel", "arbitrary"),
        name="neighbourhood_attention",
    )(*args)


def _ctx_attn_kernel(q_ref, k_ref, v_ref, o_ref):
    for p in range(N_NA_HEADS // 2):
        sl = slice(p * LANES, (p + 1) * LANES)
        o_ref[:, sl] = _softmax_pv(q_ref[:, sl], [k_ref[:, sl]], [v_ref[:, sl]], [None]).astype(o_ref.dtype)


def _context_attention(qkv, batch, ctx_len):
    blk = (ctx_len, NA_WIDTH)
    return pl.pallas_call(
        _ctx_attn_kernel,
        grid=(batch,),
        in_specs=[pl.BlockSpec(blk, lambda b: (b, 0)),
                  pl.BlockSpec(blk, lambda b: (b, 1)),
                  pl.BlockSpec(blk, lambda b: (b, 2))],
        out_specs=pl.BlockSpec(blk, lambda b: (b, 0)),
        out_shape=jax.ShapeDtypeStruct((batch * ctx_len, NA_WIDTH), BF16),
        compiler_params=_params("parallel"),
        name="context_attention",
    )(qkv, qkv, qkv)


HALO_ROWS = 16


def _merge_kernel(x_ref, mod_ref, f_ref, cv_ref, cvp_ref, cvn_ref, at_ref, gt_ref, cw_ref,
                  wf_ref, wc_ref, wa_ref, wo_ref, o_ref, *, tiles_per_seq, gate_row):
    i = pl.program_id(0)
    tm = x_ref.shape[0]
    cw = CONV_WIDTH
    has_prev = (i % tiles_per_seq) != 0
    has_next = (i % tiles_per_seq) != tiles_per_seq - 1

    u = cv_ref[:, 0:cw].astype(F32)
    gb = cv_ref[:, cw:2 * cw].astype(F32)
    gc = cv_ref[:, 2 * cw:3 * cw].astype(F32)
    z = gc * u
    zp_row = (cvp_ref[HALO_ROWS - 1:HALO_ROWS, 2 * cw:3 * cw].astype(F32)
              * cvp_ref[HALO_ROWS - 1:HALO_ROWS, 0:cw].astype(F32))
    zn_row = cvn_ref[0:1, 2 * cw:3 * cw].astype(F32) * cvn_ref[0:1, 0:cw].astype(F32)
    zp_row = jnp.where(has_prev, zp_row, 0.0)
    zn_row = jnp.where(has_next, zn_row, 0.0)
    row = lax.broadcasted_iota(jnp.int32, (tm, cw), 0)
    z_prev = jnp.where(row == 0, zp_row, pltpu.roll(z, 1, 0))
    z_next = jnp.where(row == tm - 1, zn_row, pltpu.roll(z, tm - 1, 0))
    y = cw_ref[0:1, :] * z_prev + cw_ref[1:2, :] * z + cw_ref[2:3, :] * z_next
    cv = (gb * y).astype(BF16)

    d = D_MODEL
    mix = _sigmoid(gt_ref[:, 0:d].astype(F32)) * _dot(f_ref[...], wf_ref[...])
    mix = mix + _sigmoid(gt_ref[:, d:2 * d].astype(F32)) * _dot(cv, wc_ref[...])
    mix = mix + _sigmoid(gt_ref[:, 2 * d:3 * d].astype(F32)) * _dot(at_ref[...], wa_ref[...])
    proj = _dot(mix.astype(BF16), wo_ref[...])
    o_ref[...] = x_ref[...] + mod_ref[0, gate_row:gate_row + 1, :] * proj


def _merge(x2d, mod, f, conv, attn, gates, conv_w, w_f, w_c, w_a, w_o, seq, tm, gate_row):
    m, d = x2d.shape
    tiles_per_seq = seq // tm
    halo_per_tile = tm // HALO_ROWS
    n_halo = m // HALO_ROWS
    kern = functools.partial(_merge_kernel, tiles_per_seq=tiles_per_seq, gate_row=gate_row)
    const = lambda i: (0, 0)
    return pl.pallas_call(
        kern,
        grid=(m // tm,),
        in_specs=[
            pl.BlockSpec((tm, d), lambda i: (i, 0)),
            pl.BlockSpec((1, N_MOD, d), lambda i: (i // tiles_per_seq, 0, 0)),
            pl.BlockSpec((tm, FOURIER_WIDTH), lambda i: (i, 0)),
            pl.BlockSpec((tm, 3 * CONV_WIDTH), lambda i: (i, 0)),
            pl.BlockSpec((HALO_ROWS, 3 * CONV_WIDTH),
                         lambda i: (jnp.maximum(i * halo_per_tile - 1, 0), 0)),
            pl.BlockSpec((HALO_ROWS, 3 * CONV_WIDTH),
                         lambda i: (jnp.minimum((i + 1) * halo_per_tile, n_halo - 1), 0)),
            pl.BlockSpec((tm, NA_WIDTH), lambda i: (i, 0)),
            pl.BlockSpec((tm, 3 * d), lambda i: (i, 0)),
            pl.BlockSpec((CONV_K, CONV_WIDTH), const),
            pl.BlockSpec((FOURIER_WIDTH, d), const),
            pl.BlockSpec((CONV_WIDTH, d), const),
            pl.BlockSpec((NA_WIDTH, d), const),
            pl.BlockSpec((d, d), const),
        ],
        out_specs=pl.BlockSpec((tm, d), lambda i: (i, 0)),
        out_shape=jax.ShapeDtypeStruct((m, d), F32),
        compiler_params=_params("parallel"),
        name="merge_outproj",
    )(x2d, mod, f, conv, conv, conv, attn, gates, conv_w, w_f, w_c, w_a, w_o)


def _mlp_kernel(x_ref, mod_ref, g_ref, w1_ref, w2_ref, fg_ref, o_ref, *, final_norm):
    x = x_ref[...]
    h = _norm_modulate(x, g_ref[...], mod_ref[0, 3:4, :], mod_ref[0, 4:5, :]).astype(BF16)
    acc = None
    for c0 in range(0, MLP_HIDDEN, MLP_HIDDEN_CHUNK):
        a = jnp.maximum(_dot(h, w1_ref[:, c0:c0 + MLP_HIDDEN_CHUNK]), 0.0)
        part = _dot((a * a).astype(BF16), w2_ref[c0:c0 + MLP_HIDDEN_CHUNK, :])
        acc = part if acc is None else acc + part
    y = x + mod_ref[0, 5:6, :] * acc
    if final_norm:
        ms = jnp.mean(y * y, axis=-1, keepdims=True)
        y = y * lax.rsqrt(ms + EPS) * fg_ref[...]
    o_ref[...] = y


def _mlp(x2d, mod, g, w1, w2, final_g, seq, tm, final_norm):
    m, d = x2d.shape
    tiles_per_seq = seq // tm
    const = lambda i: (0, 0)
    return pl.pallas_call(
        functools.partial(_mlp_kernel, final_norm=final_norm),
        grid=(m // tm,),
        in_specs=[
            pl.BlockSpec((tm, d), lambda i: (i, 0)),
            pl.BlockSpec((1, N_MOD, d), lambda i: (i // tiles_per_seq, 0, 0)),
            pl.BlockSpec((1, d), const),
            pl.BlockSpec((d, MLP_HIDDEN), const, pipeline_mode=pl.Buffered(1)),
            pl.BlockSpec((MLP_HIDDEN, d), const, pipeline_mode=pl.Buffered(1)),
            pl.BlockSpec((1, d), const),
        ],
        out_specs=pl.BlockSpec((tm, d), lambda i: (i, 0)),
        out_shape=jax.ShapeDtypeStruct((m, d), F32),
        compiler_params=_params("parallel"),
        name="mlp",
    )(x2d, mod, g, w1, w2, final_g)


ALL_SEGMENTS = (SEG_FOURIER, SEG_CONV, SEG_QKV, SEG_GATES)


def kernel(x, c, ctx, c_ctx, ada_w, ada_b, norm1_g, norm2_g, w_in, conv_w, rel_bias,
           w_fourier, w_conv, w_attn, w_o, mlp_w1, mlp_w2, final_g):
    batch, seq, d = x.shape
    ctx_len = ctx.shape[1]
    rows = seq // GRID_W
    depth = ada_w.shape[0]

    n_cond = batch + 1
    pad = (-n_cond) % 8
    cond = jnp.concatenate([c, c_ctx[None, :], jnp.zeros((pad, d), F32)], axis=0)
    mod_all = _modulation(cond, ada_w, ada_b)

    x2 = x.reshape(batch * seq, d)
    c2 = ctx.reshape(batch * ctx_len, d)
    fg = final_g.reshape(1, d)

    for l in range(depth):
        last = l == depth - 1
        mod_x = mod_all[l, :batch].reshape(batch, N_MOD, d)
        mod_c = jnp.broadcast_to(mod_all[l, batch].reshape(1, N_MOD, d), (batch, N_MOD, d))
        g1 = norm1_g[l].reshape(1, d)
        g2 = norm2_g[l].reshape(1, d)
        w_in_b = w_in[l].astype(BF16)
        w_f_b = w_fourier[l].astype(BF16)
        w_c_b = w_conv[l].astype(BF16)
        w_a_b = w_attn[l].astype(BF16)
        w_o_b = w_o[l].astype(BF16)
        w1_b = mlp_w1[l].astype(BF16)
        w2_b = mlp_w2[l].astype(BF16)
        bias_table = _attn_bias_table(rel_bias[l], rows)

        if last:
            (kv_c,) = _in_projection(c2, mod_c, g1, w_in_b[:, KV_START:KV_END],
                                     ((0, KV_END - KV_START),), ctx_len, ctx_len, 0, 1)
            ctx_k_col = 0
        else:
            uf_c, conv_c, qkv_c, gates_c = _in_projection(
                c2, mod_c, g1, w_in_b, ALL_SEGMENTS, ctx_len, ctx_len, 0, 1)
            kv_c, ctx_k_col = qkv_c, 1

        uf_x, conv_x, qkv_x, gates_x = _in_projection(
            x2, mod_x, g1, w_in_b, ALL_SEGMENTS, seq, TM_LATENT, 0, 1)
        f_x = _fourier_latent(uf_x, batch, seq)
        at_x = _neighbourhood_attention(qkv_x, kv_c, ctx_k_col, bias_table, batch, seq, ctx_len)
        x2 = _merge(x2, mod_x, f_x, conv_x, at_x, gates_x, conv_w[l], w_f_b, w_c_b, w_a_b, w_o_b,
                    seq, TM_LATENT, 2)

        if not last:
            f_c = _fourier_context(uf_c, batch, ctx_len)
            at_c = _context_attention(qkv_c, batch, ctx_len)
            c2 = _merge(c2, mod_c, f_c, conv_c, at_c, gates_c, conv_w[l], w_f_b, w_c_b, w_a_b, w_o_b,
                        ctx_len, ctx_len, 2)

        x2 = _mlp(x2, mod_x, g2, w1_b, w2_b, fg, seq, TM_LATENT, final_norm=last)
        if not last:
            c2 = _mlp(c2, mod_c, g2, w1_b, w2_b, fg, ctx_len, ctx_len, final_norm=False)

    return x2.reshape(batch, seq, d)
```

```python
import functools
import math

import numpy as np
import jax
import jax.numpy as jnp
from jax import lax
from jax.experimental import pallas as pl
from jax.experimental.pallas import tpu as pltpu

D_MODEL = 1024
DEPTH = 2
GRID_W = 64
HEAD_DIM = 64
N_NA_HEADS = 8
NA_WIDTH = N_NA_HEADS * HEAD_DIM
N_FOURIER_GROUPS = 4
FOURIER_WIDTH = D_MODEL // 4
FOURIER_GROUP = FOURIER_WIDTH // N_FOURIER_GROUPS
CONV_WIDTH = D_MODEL // 4
CONV_K = 3
WIN_ROWS = 8
WIN_COLS = 16
MLP_HIDDEN = 4 * D_MODEL
N_MOD = 6
EPS = 1e-6
NEG = -1e30
LOG2E = math.log2(math.e)

SEG_FOURIER = (0, FOURIER_WIDTH)
SEG_CONV = (SEG_FOURIER[1], SEG_FOURIER[1] + 3 * CONV_WIDTH)
SEG_QKV = (SEG_CONV[1], SEG_CONV[1] + 3 * NA_WIDTH)
SEG_GATES = (SEG_QKV[1], SEG_QKV[1] + 3 * D_MODEL)
KV_START = SEG_QKV[0] + NA_WIDTH
KV_END = SEG_QKV[1]

LANES = 128
MXU_DIM = 256
VMEM_LIMIT_BYTES = 56 * 1024 * 1024

TM_LATENT = 512
ROWS_PER_ATTN_STEP = 4
ATTN_Q = ROWS_PER_ATTN_STEP * GRID_W
ATTN_KEY_BLOCKS = 3
DFT_N1 = 64
DFT_LANE_TILE = 4096
DFT_K1_PER_STEP = 8
MOD_COL_TILE = 1536
PROJ_COL_CHUNK = 512
MLP_HIDDEN_CHUNK = 1024
MERGE_ROW_CHUNK = 256

BF16 = jnp.bfloat16
F32 = jnp.float32


def _params(*semantics):
    return pltpu.CompilerParams(dimension_semantics=semantics, vmem_limit_bytes=VMEM_LIMIT_BYTES)


def _dot(a, b):
    return jnp.dot(a, b, preferred_element_type=F32)


def _dot_nt(a, b):
    return lax.dot_general(a, b, (((1,), (1,)), ((), ())), preferred_element_type=F32)


def _channel_dft_tables():
    c = np.arange(FOURIER_GROUP)
    ang = 2.0 * np.pi * ((c[:, None] * c[None, :]) % FOURIER_GROUP) / FOURIER_GROUP
    eye = np.eye(N_FOURIER_GROUPS)
    s = 1.0 / math.sqrt(FOURIER_GROUP)
    return np.stack([np.kron(eye, np.cos(ang) * s), np.kron(eye, np.sin(ang) * s)]).astype(np.float32)


def _latent_dft_tables(n):
    n1, n2 = DFT_N1, n // DFT_N1
    a = np.arange(n1)
    ang1 = 2.0 * np.pi * ((a[:, None] * a[None, :]) % n1) / n1
    s1 = 1.0 / math.sqrt(n1)
    f1 = np.concatenate([np.cos(ang1) * s1, -np.sin(ang1) * s1], axis=0)
    k1 = np.arange(n1)[:, None, None]
    k2 = np.arange(n2)[None, :, None]
    m2 = np.arange(n2)[None, None, :]
    ang = 2.0 * np.pi * ((m2 * (k1 + n1 * k2)) % n) / n
    s2 = 1.0 / math.sqrt(n2)
    tr, ti = np.cos(ang) * s2, -np.sin(ang) * s2
    g = np.concatenate([np.concatenate([tr, -ti], axis=2),
                        np.concatenate([ti, tr], axis=2)], axis=1)
    return f1.astype(np.float32), g.astype(np.float32)


def _context_dft_table(n):
    a = np.arange(n)
    ang = 2.0 * np.pi * ((a[:, None] * a[None, :]) % n) / n
    s = 1.0 / math.sqrt(n)
    return np.concatenate([np.cos(ang) * s, -np.sin(ang) * s], axis=0).astype(np.float32)


def _attn_key_block_start(rb, n_row_blocks):
    return jnp.clip(rb - 1, 0, n_row_blocks - ATTN_KEY_BLOCKS)


def _attn_bias_table(rel_bias, rows):
    depth, h, ndr, _ = rel_bias.shape
    w = GRID_W
    rb = rel_bias * LOG2E
    ends = w - WIN_COLS
    ext = jnp.concatenate([
        jnp.broadcast_to(rb[..., :1], (depth, h, ndr, ends)), rb,
        jnp.broadcast_to(rb[..., -1:], (depth, h, ndr, ends)), jnp.zeros((depth, h, ndr, 1), F32)], axis=-1)
    skew = jnp.tile(ext, (1, 1, 1, w))[..., :w * (2 * w - 1)].reshape(depth, h, ndr, w, 2 * w - 1)
    slabs = skew[..., w - 1:]
    qc = np.arange(w)
    cs = np.clip(qc - WIN_COLS // 2, 0, w - WIN_COLS)[:, None]
    col_ok = (qc[None, :] >= cs) & (qc[None, :] < cs + WIN_COLS)
    slabs = jnp.where(col_ok, slabs, NEG)
    slabs = jnp.transpose(slabs, (0, 1, 3, 2, 4))

    rq, nkr = ROWS_PER_ATTN_STEP, ROWS_PER_ATTN_STEP * ATTN_KEY_BLOCKS
    variants = [(0, 0), (rq, 0), (rows - rq, rows - nkr)]

    def masked(n):
        return jnp.full((depth, h, w, n, w), NEG, F32)

    tables = []
    for r0, ks in variants:
        q_rows = []
        for qi in range(rq):
            qr = r0 + qi
            rs = min(max(qr - WIN_ROWS // 2, 0), rows - WIN_ROWS)
            lead = rs - ks
            d0 = rs - qr + WIN_ROWS - 1
            pieces = [masked(lead), slabs[:, :, :, d0:d0 + WIN_ROWS], masked(nkr - WIN_ROWS - lead)]
            row = jnp.concatenate([p for p in pieces if p.shape[3] > 0], axis=3)
            q_rows.append(row.reshape(depth, h, w, nkr * w))
        tables.append(jnp.concatenate(q_rows, axis=2))
    return jnp.stack(tables, axis=1)


def _mod_kernel(c_ref, w_ref, b_ref, o_ref):
    c = c_ref[...]
    s = c * (1.0 / (1.0 + jnp.exp(-c)))
    o_ref[...] = jnp.dot(s, w_ref[...], precision=lax.Precision.HIGHEST,
                         preferred_element_type=F32) + b_ref[...]


def _modulation(cond, ada_w, ada_b):
    r, d = cond.shape
    depth, _, n = ada_w.shape
    tn = MOD_COL_TILE
    return pl.pallas_call(
        _mod_kernel,
        grid=(depth, n // tn),
        in_specs=[
            pl.BlockSpec((r, d), lambda l, j: (0, 0)),
            pl.BlockSpec((None, d, tn), lambda l, j: (l, 0, j)),
            pl.BlockSpec((None, 1, tn), lambda l, j: (l, 0, j)),
        ],
        out_specs=pl.BlockSpec((None, r, tn), lambda l, j: (l, 0, j)),
        out_shape=jax.ShapeDtypeStruct((depth, r, n), F32),
        compiler_params=_params("parallel", "parallel"),
        name="adaln_modulation",
    )(cond, ada_w, ada_b.reshape(depth, 1, n))


def _norm_modulate(x, g, shift, scale):
    ms = jnp.mean(x * x, axis=-1, keepdims=True)
    y = x * lax.rsqrt(ms + EPS) * g
    return y * (1.0 + scale) + shift


def _twice_sigmoid(g_bf16):
    return jnp.tanh((g_bf16 * 0.5).astype(F32)) + 1.0


def _inproj_kernel(x_ref, mod_ref, g_ref, w_ref, *out_refs, segments, shift_row, scale_row):
    h = _norm_modulate(x_ref[...], g_ref[...],
                       mod_ref[0, shift_row:shift_row + 1, :], mod_ref[0, scale_row:scale_row + 1, :])
    hb = h.astype(BF16)
    for o_ref, (lo, hi) in zip(out_refs, segments):
        for c0 in range(lo, hi, PROJ_COL_CHUNK):
            c1 = min(c0 + PROJ_COL_CHUNK, hi)
            o_ref[:, c0 - lo:c1 - lo] = _dot(hb, w_ref[:, c0:c1]).astype(o_ref.dtype)


def _in_projection(x2d, mod, g, w_bf16, segments, seq, tm, shift_row, scale_row):
    m, d = x2d.shape
    n = w_bf16.shape[1]
    tiles_per_seq = seq // tm
    kern = functools.partial(_inproj_kernel, segments=tuple(segments),
                             shift_row=shift_row, scale_row=scale_row)
    return pl.pallas_call(
        kern,
        grid=(m // tm,),
        in_specs=[
            pl.BlockSpec((tm, d), lambda i: (i, 0)),
            pl.BlockSpec((1, N_MOD, d), lambda i: (i // tiles_per_seq, 0, 0)),
            pl.BlockSpec((1, d), lambda i: (0, 0)),
            pl.BlockSpec((d, n), lambda i: (0, 0), pipeline_mode=pl.Buffered(1)),
        ],
        out_specs=[pl.BlockSpec((tm, hi - lo), lambda i: (i, 0)) for lo, hi in segments],
        out_shape=[jax.ShapeDtypeStruct((m, hi - lo), BF16) for lo, hi in segments],
        compiler_params=_params("parallel"),
        name="norm_inproj",
    )(x2d, mod, g, w_bf16)


def _dft_stage1_kernel(x_ref, f_ref, o_ref):
    o_ref[...] = _dot(f_ref[...].astype(BF16), x_ref[...]).astype(o_ref.dtype)


def _dft_stage2_kernel(z_ref, g_ref, cs_ref, o_ref, *, k1_per_step, n2, c):
    cos_t, sin_t = cs_ref[0].astype(BF16), cs_ref[1].astype(BF16)
    for j in range(k1_per_step):
        zz = z_ref[:, j].reshape(2 * n2, c)
        p = _dot(g_ref[j].astype(BF16), zz).astype(BF16)
        y = _dot(p[:n2], cos_t) + _dot(p[n2:], sin_t)
        o_ref[:, j * c:(j + 1) * c] = y.astype(o_ref.dtype)


def _fourier_latent(u, batch, seq):
    c = u.shape[1]
    n1, n2 = DFT_N1, seq // DFT_N1
    f1_np, g_np = _latent_dft_tables(seq)
    f1 = jnp.asarray(f1_np)
    g = jnp.asarray(g_np)
    cs = jnp.asarray(_channel_dft_tables())
    lt = min(DFT_LANE_TILE, n2 * c)
    z = pl.pallas_call(
        _dft_stage1_kernel,
        grid=(batch, (n2 * c) // lt),
        in_specs=[
            pl.BlockSpec((None, n1, lt), lambda b, j: (b, 0, j)),
            pl.BlockSpec((2 * n1, n1), lambda b, j: (0, 0)),
        ],
        out_specs=pl.BlockSpec((None, 2 * n1, lt), lambda b, j: (b, 0, j)),
        out_shape=jax.ShapeDtypeStruct((batch, 2 * n1, n2 * c), BF16),
        compiler_params=_params("parallel", "parallel"),
        name="dft_stage1",
    )(u.reshape(batch, n1, n2 * c), f1)
    kb = DFT_K1_PER_STEP
    kern = functools.partial(_dft_stage2_kernel, k1_per_step=kb, n2=n2, c=c)
    y = pl.pallas_call(
        kern,
        grid=(n1 // kb, batch),
        in_specs=[
            pl.BlockSpec((None, 2, kb, n2, c), lambda k, b: (b, 0, k, 0, 0)),
            pl.BlockSpec((kb, 2 * n2, 2 * n2), lambda k, b: (k, 0, 0)),
            pl.BlockSpec((2, c, c), lambda k, b: (0, 0, 0)),
        ],
        out_specs=pl.BlockSpec((None, n2, kb * c), lambda k, b: (b, 0, k)),
        out_shape=jax.ShapeDtypeStruct((batch, n2, n1 * c), BF16),
        compiler_params=_params("parallel", "parallel"),
        name="dft_stage2",
    )(z.reshape(batch, 2, n1, n2, c), g, cs)
    return y.reshape(batch * seq, c)


def _dft_context_kernel(u_ref, f_ref, cs_ref, o_ref, *, n):
    p = _dot(f_ref[...].astype(BF16), u_ref[...]).astype(BF16)
    y = _dot(p[:n], cs_ref[0].astype(BF16)) + _dot(p[n:], cs_ref[1].astype(BF16))
    o_ref[...] = y.astype(o_ref.dtype)


def _fourier_context(u, batch, seq):
    c = u.shape[1]
    f = jnp.asarray(_context_dft_table(seq))
    cs = jnp.asarray(_channel_dft_tables())
    return pl.pallas_call(
        functools.partial(_dft_context_kernel, n=seq),
        grid=(batch,),
        in_specs=[
            pl.BlockSpec((seq, c), lambda b: (b, 0)),
            pl.BlockSpec((2 * seq, seq), lambda b: (0, 0)),
            pl.BlockSpec((2, c, c), lambda b: (0, 0, 0)),
        ],
        out_specs=pl.BlockSpec((seq, c), lambda b: (b, 0)),
        out_shape=jax.ShapeDtypeStruct((batch * seq, c), BF16),
        compiler_params=_params("parallel"),
        name="dft_context",
    )(u, f, cs)


def _attend_heads(q_ref, key_refs, value_refs, bias_fn, o_ref):
    q_scale = HEAD_DIM ** -0.5 * LOG2E

    def lanes(h):
        return slice((h // 2) * LANES, (h // 2 + 1) * LANES)

    def own_lanes(shape, h):
        low = lax.broadcasted_iota(jnp.int32, shape, 1) < HEAD_DIM
        return low if h % 2 == 0 else jnp.logical_not(low)

    def logits(h):
        q_pair = q_ref[:, lanes(h)]
        qs = (q_pair.astype(F32) * q_scale).astype(BF16)
        qm = jnp.where(own_lanes(qs.shape, h), qs, jnp.zeros_like(qs))
        out = []
        for j, k_ref in enumerate(key_refs):
            sj = _dot_nt(qm, k_ref[:, lanes(h)])
            bj = bias_fn(h, j)
            out.append(sj if bj is None else sj + bj)
        return out

    s_next = logits(0)
    even_out = None
    for h in range(N_NA_HEADS):
        s = s_next
        if h + 1 < N_NA_HEADS:
            s_next = logits(h + 1)
        m = s[0]
        for sj in s[1:]:
            m = jnp.maximum(m, sj)
        m = jnp.max(m, axis=-1, keepdims=True)
        p = jnp.concatenate([jnp.exp2(sj - m).astype(BF16) for sj in s], axis=1)
        v = jnp.concatenate([v_ref[:, lanes(h)] for v_ref in value_refs], axis=0)
        v_aug = jnp.where(own_lanes(v.shape, h), v, jnp.ones_like(v))
        acc = _dot(p, v_aug)
        denom = pltpu.roll(acc, HEAD_DIM, 1)
        out = acc * (1.0 / denom)
        if h % 2 == 0:
            even_out = out
        else:
            o_ref[:, lanes(h)] = jnp.where(own_lanes(out.shape, 0), even_out, out).astype(o_ref.dtype)


def _natten_kernel(q_ref, *refs):
    nb = ATTN_KEY_BLOCKS
    k_refs, v_refs = refs[:nb], refs[nb:2 * nb]
    kc_ref, vc_ref, bias_ref, o_ref = refs[2 * nb:]
    kw = ATTN_Q

    def bias_fn(h, j):
        return bias_ref[h, :, j * kw:(j + 1) * kw] if j < nb else None

    _attend_heads(q_ref, list(k_refs) + [kc_ref], list(v_refs) + [vc_ref], bias_fn, o_ref)


def _neighbourhood_attention(qkv, kv_ctx, ctx_k_col, bias_tables, layer, batch, seq, ctx_len):
    m = qkv.shape[0]
    tq = ATTN_Q
    nrb = seq // tq
    assert ctx_len == tq, "context keys are processed as one key block"

    def q_map(b, r):
        return (b * nrb + r, 0)

    def kv_map(j, col):
        return lambda b, r: (b * nrb + _attn_key_block_start(r, nrb) + j, col)

    def bias_map(b, r):
        return (layer, jnp.where(r == 0, 0, jnp.where(r == nrb - 1, 2, 1)), 0, 0, 0)

    blk = (tq, NA_WIDTH)
    in_specs = [pl.BlockSpec(blk, q_map)]
    in_specs += [pl.BlockSpec(blk, kv_map(j, 1)) for j in range(ATTN_KEY_BLOCKS)]
    in_specs += [pl.BlockSpec(blk, kv_map(j, 2)) for j in range(ATTN_KEY_BLOCKS)]
    in_specs += [
        pl.BlockSpec((ctx_len, NA_WIDTH), lambda b, r: (b, ctx_k_col)),
        pl.BlockSpec((ctx_len, NA_WIDTH), lambda b, r: (b, ctx_k_col + 1)),
        pl.BlockSpec((None, None, N_NA_HEADS, tq, ATTN_KEY_BLOCKS * tq), bias_map),
    ]
    args = [qkv] * (1 + 2 * ATTN_KEY_BLOCKS) + [kv_ctx, kv_ctx, bias_tables]
    return pl.pallas_call(
        _natten_kernel,
        grid=(batch, nrb),
        in_specs=in_specs,
        out_specs=pl.BlockSpec(blk, q_map),
        out_shape=jax.ShapeDtypeStruct((m, NA_WIDTH), BF16),
        compiler_params=_params("parallel", "arbitrary"),
        name="neighbourhood_attention",
    )(*args)


def _ctx_attn_kernel(q_ref, k_ref, v_ref, o_ref):
    _attend_heads(q_ref, [k_ref], [v_ref], lambda h, j: None, o_ref)


def _context_attention(qkv, batch, ctx_len):
    blk = (ctx_len, NA_WIDTH)
    return pl.pallas_call(
        _ctx_attn_kernel,
        grid=(batch,),
        in_specs=[pl.BlockSpec(blk, lambda b: (b, 0)),
                  pl.BlockSpec(blk, lambda b: (b, 1)),
                  pl.BlockSpec(blk, lambda b: (b, 2))],
        out_specs=pl.BlockSpec(blk, lambda b: (b, 0)),
        out_shape=jax.ShapeDtypeStruct((batch * ctx_len, NA_WIDTH), BF16),
        compiler_params=_params("parallel"),
        name="context_attention",
    )(qkv, qkv, qkv)


HALO_ROWS = 16


def _merge_kernel(x_ref, mod_ref, f_ref, cv_ref, cvp_ref, cvn_ref, at_ref, gt_ref, cw_ref,
                  wf_ref, wc_ref, wa_ref, wo_ref, o_ref, cv_scr, *, tiles_per_seq, gate_row):
    i = pl.program_id(0)
    tm = x_ref.shape[0]
    cw = CONV_WIDTH
    has_prev = (i % tiles_per_seq) != 0
    has_next = (i % tiles_per_seq) != tiles_per_seq - 1

    u = cv_ref[:, 0:cw].astype(F32)
    gb = cv_ref[:, cw:2 * cw].astype(F32)
    gc = cv_ref[:, 2 * cw:3 * cw].astype(F32)
    z = gc * u
    zp_row = (cvp_ref[HALO_ROWS - 1:HALO_ROWS, 2 * cw:3 * cw].astype(F32)
              * cvp_ref[HALO_ROWS - 1:HALO_ROWS, 0:cw].astype(F32))
    zn_row = cvn_ref[0:1, 2 * cw:3 * cw].astype(F32) * cvn_ref[0:1, 0:cw].astype(F32)
    zp_row = jnp.where(has_prev, zp_row, 0.0)
    zn_row = jnp.where(has_next, zn_row, 0.0)
    row = lax.broadcasted_iota(jnp.int32, (tm, cw), 0)
    z_prev = jnp.where(row == 0, zp_row, pltpu.roll(z, 1, 0))
    z_next = jnp.where(row == tm - 1, zn_row, pltpu.roll(z, tm - 1, 0))
    y = cw_ref[0:1, :] * z_prev + cw_ref[1:2, :] * z + cw_ref[2:3, :] * z_next
    cv_scr[...] = (gb * y).astype(BF16)

    d = D_MODEL
    gate = mod_ref[0, gate_row:gate_row + 1, :]

    def branches(c):
        r = slice(c * MERGE_ROW_CHUNK, (c + 1) * MERGE_ROW_CHUNK)
        return (_dot(f_ref[r, :], wf_ref[...]), _dot(cv_scr[r, :], wc_ref[...]),
                _dot(at_ref[r, :], wa_ref[...]))

    nxt = branches(0)
    n_chunks = tm // MERGE_ROW_CHUNK
    for c in range(n_chunks):
        r = slice(c * MERGE_ROW_CHUNK, (c + 1) * MERGE_ROW_CHUNK)
        pf, pc, pa = nxt
        if c + 1 < n_chunks:
            nxt = branches(c + 1)
        mix = _twice_sigmoid(gt_ref[r, 0:d]) * pf
        mix = mix + _twice_sigmoid(gt_ref[r, d:2 * d]) * pc
        mix = mix + _twice_sigmoid(gt_ref[r, 2 * d:3 * d]) * pa
        proj = _dot(mix.astype(BF16), wo_ref[...])
        o_ref[r, :] = x_ref[r, :] + (0.5 * gate) * proj


def _merge(x2d, mod, f, conv, attn, gates, conv_w, w_f, w_c, w_a, w_o, seq, tm, gate_row):
    m, d = x2d.shape
    tiles_per_seq = seq // tm
    halo_per_tile = tm // HALO_ROWS
    n_halo = m // HALO_ROWS
    kern = functools.partial(_merge_kernel, tiles_per_seq=tiles_per_seq, gate_row=gate_row)
    const = lambda i: (0, 0)
    return pl.pallas_call(
        kern,
        grid=(m // tm,),
        in_specs=[
            pl.BlockSpec((tm, d), lambda i: (i, 0)),
            pl.BlockSpec((1, N_MOD, d), lambda i: (i // tiles_per_seq, 0, 0)),
            pl.BlockSpec((tm, FOURIER_WIDTH), lambda i: (i, 0)),
            pl.BlockSpec((tm, 3 * CONV_WIDTH), lambda i: (i, 0)),
            pl.BlockSpec((HALO_ROWS, 3 * CONV_WIDTH),
                         lambda i: (jnp.maximum(i * halo_per_tile - 1, 0), 0)),
            pl.BlockSpec((HALO_ROWS, 3 * CONV_WIDTH),
                         lambda i: (jnp.minimum((i + 1) * halo_per_tile, n_halo - 1), 0)),
            pl.BlockSpec((tm, NA_WIDTH), lambda i: (i, 0)),
            pl.BlockSpec((tm, 3 * d), lambda i: (i, 0)),
            pl.BlockSpec((CONV_K, CONV_WIDTH), const),
            pl.BlockSpec((FOURIER_WIDTH, d), const),
            pl.BlockSpec((CONV_WIDTH, d), const),
            pl.BlockSpec((NA_WIDTH, d), const),
            pl.BlockSpec((d, d), const),
        ],
        out_specs=pl.BlockSpec((tm, d), lambda i: (i, 0)),
        out_shape=jax.ShapeDtypeStruct((m, d), F32),
        scratch_shapes=[pltpu.VMEM((tm, CONV_WIDTH), BF16)],
        compiler_params=_params("parallel"),
        name="merge_outproj",
    )(x2d, mod, f, conv, conv, conv, attn, gates, conv_w, w_f, w_c, w_a, w_o)


def _mlp_kernel(x_ref, mod_ref, g_ref, w1_ref, w2_ref, fg_ref, o_ref, *, final_norm):
    x = x_ref[...]
    h = _norm_modulate(x, g_ref[...], mod_ref[0, 3:4, :], mod_ref[0, 4:5, :]).astype(BF16)
    acc = None
    for c0 in range(0, MLP_HIDDEN, MLP_HIDDEN_CHUNK):
        a = jnp.maximum(_dot(h, w1_ref[:, c0:c0 + MLP_HIDDEN_CHUNK]), 0.0)
        part = _dot((a * a).astype(BF16), w2_ref[c0:c0 + MLP_HIDDEN_CHUNK, :])
        acc = part if acc is None else acc + part
    y = x + mod_ref[0, 5:6, :] * acc
    if final_norm:
        ms = jnp.mean(y * y, axis=-1, keepdims=True)
        y = y * lax.rsqrt(ms + EPS) * fg_ref[...]
    o_ref[...] = y


def _mlp(x2d, mod, g, w1, w2, final_g, seq, tm, final_norm):
    m, d = x2d.shape
    tiles_per_seq = seq // tm
    const = lambda i: (0, 0)
    return pl.pallas_call(
        functools.partial(_mlp_kernel, final_norm=final_norm),
        grid=(m // tm,),
        in_specs=[
            pl.BlockSpec((tm, d), lambda i: (i, 0)),
            pl.BlockSpec((1, N_MOD, d), lambda i: (i // tiles_per_seq, 0, 0)),
            pl.BlockSpec((1, d), const),
            pl.BlockSpec((d, MLP_HIDDEN), const, pipeline_mode=pl.Buffered(1)),
            pl.BlockSpec((MLP_HIDDEN, d), const, pipeline_mode=pl.Buffered(1)),
            pl.BlockSpec((1, d), const),
        ],
        out_specs=pl.BlockSpec((tm, d), lambda i: (i, 0)),
        out_shape=jax.ShapeDtypeStruct((m, d), F32),
        compiler_params=_params("parallel"),
        name="mlp",
    )(x2d, mod, g, w1, w2, final_g)


ALL_SEGMENTS = (SEG_FOURIER, SEG_CONV, SEG_QKV, SEG_GATES)


def kernel(x, c, ctx, c_ctx, ada_w, ada_b, norm1_g, norm2_g, w_in, conv_w, rel_bias,
           w_fourier, w_conv, w_attn, w_o, mlp_w1, mlp_w2, final_g):
    batch, seq, d = x.shape
    ctx_len = ctx.shape[1]
    rows = seq // GRID_W
    depth = ada_w.shape[0]

    n_cond = batch + 1
    pad = (-n_cond) % 8
    cond = jnp.concatenate([c, c_ctx[None, :], jnp.zeros((pad, d), F32)], axis=0)
    mod_all = _modulation(cond, ada_w, ada_b)

    x2 = x.reshape(batch * seq, d)
    c2 = ctx.reshape(batch * ctx_len, d)
    fg = final_g.reshape(1, d)
    bias_tables = _attn_bias_table(rel_bias, rows)

    for l in range(depth):
        last = l == depth - 1
        mod_x = mod_all[l, :batch].reshape(batch, N_MOD, d)
        mod_c = jnp.broadcast_to(mod_all[l, batch].reshape(1, N_MOD, d), (batch, N_MOD, d))
        g1 = norm1_g[l].reshape(1, d)
        g2 = norm2_g[l].reshape(1, d)
        w_in_b = w_in[l].astype(BF16)
        w_f_b = w_fourier[l].astype(BF16)
        w_c_b = w_conv[l].astype(BF16)
        w_a_b = w_attn[l].astype(BF16)
        w_o_b = w_o[l].astype(BF16)
        w1_b = mlp_w1[l].astype(BF16)
        w2_b = mlp_w2[l].astype(BF16)

        if last:
            (kv_c,) = _in_projection(c2, mod_c, g1, w_in_b[:, KV_START:KV_END],
                                     ((0, KV_END - KV_START),), ctx_len, ctx_len, 0, 1)
            ctx_k_col = 0
        else:
            uf_c, conv_c, qkv_c, gates_c = _in_projection(
                c2, mod_c, g1, w_in_b, ALL_SEGMENTS, ctx_len, ctx_len, 0, 1)
            kv_c, ctx_k_col = qkv_c, 1

        uf_x, conv_x, qkv_x, gates_x = _in_projection(
            x2, mod_x, g1, w_in_b, ALL_SEGMENTS, seq, TM_LATENT, 0, 1)
        f_x = _fourier_latent(uf_x, batch, seq)
        at_x = _neighbourhood_attention(qkv_x, kv_c, ctx_k_col, bias_tables, l, batch, seq, ctx_len)
        x2 = _merge(x2, mod_x, f_x, conv_x, at_x, gates_x, conv_w[l], w_f_b, w_c_b, w_a_b, w_o_b,
                    seq, TM_LATENT, 2)

        if not last:
            f_c = _fourier_context(uf_c, batch, ctx_len)
            at_c = _context_attention(qkv_c, batch, ctx_len)
            c2 = _merge(c2, mod_c, f_c, conv_c, at_c, gates_c, conv_w[l], w_f_b, w_c_b, w_a_b, w_o_b,
                        ctx_len, ctx_len, 2)

        x2 = _mlp(x2, mod_x, g2, w1_b, w2_b, fg, seq, TM_LATENT, final_norm=last)
        if not last:
            c2 = _mlp(c2, mod_c, g2, w1_b, w2_b, fg, ctx_len, ctx_len, final_norm=False)

    return x2.reshape(batch, seq, d)
```

```python
import functools
import math

import numpy as np
import jax
import jax.numpy as jnp
from jax import lax
from jax.experimental import pallas as pl
from jax.experimental.pallas import tpu as pltpu

D_MODEL = 1024
DEPTH = 2
GRID_W = 64
HEAD_DIM = 64
N_NA_HEADS = 8
NA_WIDTH = N_NA_HEADS * HEAD_DIM
N_FOURIER_GROUPS = 4
FOURIER_WIDTH = D_MODEL // 4
FOURIER_GROUP = FOURIER_WIDTH // N_FOURIER_GROUPS
CONV_WIDTH = D_MODEL // 4
CONV_K = 3
WIN_ROWS = 8
WIN_COLS = 16
MLP_HIDDEN = 4 * D_MODEL
N_MOD = 6
EPS = 1e-6
NEG = -1e30
LOG2E = math.log2(math.e)

SEG_FOURIER = (0, FOURIER_WIDTH)
SEG_CONV = (SEG_FOURIER[1], SEG_FOURIER[1] + 3 * CONV_WIDTH)
SEG_QKV = (SEG_CONV[1], SEG_CONV[1] + 3 * NA_WIDTH)
SEG_GATES = (SEG_QKV[1], SEG_QKV[1] + 3 * D_MODEL)
KV_START = SEG_QKV[0] + NA_WIDTH
KV_END = SEG_QKV[1]

LANES = 128
MXU_DIM = 256
VMEM_LIMIT_BYTES = 56 * 1024 * 1024

TM_LATENT = 512
ROWS_PER_ATTN_STEP = 4
ATTN_Q = ROWS_PER_ATTN_STEP * GRID_W
ATTN_KEY_BLOCKS = 3
DFT_N1 = 64
DFT_ROWS_PER_STEP = 8
MOD_COL_TILE = 1536
PROJ_COL_CHUNK = 512
MLP_HIDDEN_CHUNK = 1024
MERGE_ROW_CHUNK = 256

BF16 = jnp.bfloat16
F32 = jnp.float32


def _params(*semantics):
    return pltpu.CompilerParams(dimension_semantics=semantics, vmem_limit_bytes=VMEM_LIMIT_BYTES)


def _dot(a, b):
    return jnp.dot(a, b, preferred_element_type=F32)


def _dot_nt(a, b):
    return lax.dot_general(a, b, (((1,), (1,)), ((), ())), preferred_element_type=F32)


def _channel_dft_tables():
    c = np.arange(FOURIER_GROUP)
    ang = 2.0 * np.pi * ((c[:, None] * c[None, :]) % FOURIER_GROUP) / FOURIER_GROUP
    eye = np.eye(N_FOURIER_GROUPS)
    s = 1.0 / math.sqrt(FOURIER_GROUP)
    return np.stack([np.kron(eye, np.cos(ang) * s), np.kron(eye, np.sin(ang) * s)]).astype(np.float32)


def _latent_dft_tables(n):
    n1, n2 = DFT_N1, n // DFT_N1
    a = np.arange(n1)
    ang1 = 2.0 * np.pi * ((a[:, None] * a[None, :]) % n1) / n1
    s1 = 1.0 / math.sqrt(n1)
    f1 = np.concatenate([np.cos(ang1) * s1, -np.sin(ang1) * s1], axis=0)
    k1 = np.arange(n1)[:, None, None]
    k2 = np.arange(n2)[None, :, None]
    m2 = np.arange(n2)[None, None, :]
    ang = 2.0 * np.pi * ((m2 * (k1 + n1 * k2)) % n) / n
    s2 = 1.0 / math.sqrt(n2)
    tr, ti = np.cos(ang) * s2, -np.sin(ang) * s2
    g = np.concatenate([np.concatenate([tr, -ti], axis=2),
                        np.concatenate([ti, tr], axis=2)], axis=1)
    return f1.astype(np.float32), g.astype(np.float32)


def _context_dft_table(n):
    a = np.arange(n)
    ang = 2.0 * np.pi * ((a[:, None] * a[None, :]) % n) / n
    s = 1.0 / math.sqrt(n)
    return np.concatenate([np.cos(ang) * s, -np.sin(ang) * s], axis=0).astype(np.float32)


def _attn_key_block_start(rb, n_row_blocks):
    return jnp.clip(rb - 1, 0, n_row_blocks - ATTN_KEY_BLOCKS)


def _attn_bias_table(rel_bias, rows):
    depth, h, ndr, _ = rel_bias.shape
    w = GRID_W
    rq, nkr = ROWS_PER_ATTN_STEP, ROWS_PER_ATTN_STEP * ATTN_KEY_BLOCKS
    ends = w - WIN_COLS
    ext = jnp.concatenate([
        jnp.broadcast_to(rel_bias[..., :1], (depth, h, ndr, ends)), rel_bias,
        jnp.broadcast_to(rel_bias[..., -1:], (depth, h, ndr, ends)), jnp.zeros((depth, h, ndr, 1), F32)], axis=-1)
    return pl.pallas_call(
        functools.partial(_bias_table_kernel, rows=rows),
        grid=(depth, h),
        in_specs=[pl.BlockSpec((None, None, ndr, 2 * w), lambda l, hh: (l, hh, 0, 0))],
        out_specs=pl.BlockSpec((None, 3, None, rq * w, nkr * w), lambda l, hh: (l, 0, hh, 0, 0)),
        out_shape=jax.ShapeDtypeStruct((depth, 3, h, rq * w, nkr * w), F32),
        compiler_params=_params("parallel", "parallel"),
        name="attn_bias_table",
    )(ext)


def _bias_table_kernel(ext_ref, o_ref, *, rows):
    w = GRID_W
    rq, nkr = ROWS_PER_ATTN_STEP, ROWS_PER_ATTN_STEP * ATTN_KEY_BLOCKS
    q = lax.broadcasted_iota(jnp.int32, (w, 2 * w), 0)
    t = lax.broadcasted_iota(jnp.int32, (w, 2 * w), 1)
    first_col = jnp.clip(q - WIN_COLS // 2, 0, w - WIN_COLS)
    neg = jnp.full((w, 2 * w), NEG, F32)
    left, right = [], []
    for d in range(2 * WIN_ROWS - 1):
        row = jnp.broadcast_to(ext_ref[d:d + 1, :] * LOG2E, (w, 2 * w))
        lo = pltpu.roll(row, w + 1, 1, stride=1, stride_axis=0)
        hi = pltpu.roll(row, 1, 1, stride=1, stride_axis=0)
        left.append(jnp.where((t >= first_col) & (t < first_col + WIN_COLS), lo, neg))
        right.append(jnp.where((t - w >= first_col) & (t - w < first_col + WIN_COLS), hi, neg))

    variants = [(0, 0), (rq, 0), (rows - rq, rows - nkr)]
    for v, (r0, ks) in enumerate(variants):
        for qi in range(rq):
            qr = r0 + qi
            rs = min(max(qr - WIN_ROWS // 2, 0), rows - WIN_ROWS)
            lead = rs - ks
            d0 = rs - qr + WIN_ROWS - 1

            def slab(kj, side):
                return side[d0 + kj - lead] if lead <= kj < lead + WIN_ROWS else neg

            for p in range(nkr // 2):
                tile = jnp.where(t < w, slab(2 * p, left), slab(2 * p + 1, right))
                o_ref[v, qi * w:(qi + 1) * w, p * 2 * w:(p + 1) * 2 * w] = tile


def _mod_kernel(c_ref, w_ref, b_ref, o_ref):
    c = c_ref[...]
    s = c * (1.0 / (1.0 + jnp.exp(-c)))
    o_ref[...] = jnp.dot(s, w_ref[...], precision=lax.Precision.HIGHEST,
                         preferred_element_type=F32) + b_ref[...]


def _modulation(cond, ada_w, ada_b):
    r, d = cond.shape
    depth, _, n = ada_w.shape
    tn = MOD_COL_TILE
    return pl.pallas_call(
        _mod_kernel,
        grid=(depth, n // tn),
        in_specs=[
            pl.BlockSpec((r, d), lambda l, j: (0, 0)),
            pl.BlockSpec((None, d, tn), lambda l, j: (l, 0, j)),
            pl.BlockSpec((None, 1, tn), lambda l, j: (l, 0, j)),
        ],
        out_specs=pl.BlockSpec((None, r, tn), lambda l, j: (l, 0, j)),
        out_shape=jax.ShapeDtypeStruct((depth, r, n), F32),
        compiler_params=_params("parallel", "parallel"),
        name="adaln_modulation",
    )(cond, ada_w, ada_b.reshape(depth, 1, n))


def _norm_modulate(x, g, shift, scale):
    ms = jnp.mean(x * x, axis=-1, keepdims=True)
    y = x * lax.rsqrt(ms + EPS) * g
    return y * (1.0 + scale) + shift


def _twice_sigmoid(g_bf16):
    return jnp.tanh((g_bf16 * 0.5).astype(F32)) + 1.0


def _inproj_kernel(x_ref, mod_ref, g_ref, w_ref, *out_refs, segments, shift_row, scale_row):
    h = _norm_modulate(x_ref[...], g_ref[...],
                       mod_ref[0, shift_row:shift_row + 1, :], mod_ref[0, scale_row:scale_row + 1, :])
    hb = h.astype(BF16)
    for o_ref, (lo, hi) in zip(out_refs, segments):
        for c0 in range(lo, hi, PROJ_COL_CHUNK):
            c1 = min(c0 + PROJ_COL_CHUNK, hi)
            o_ref[:, c0 - lo:c1 - lo] = _dot(hb, w_ref[:, c0:c1]).astype(o_ref.dtype)


def _in_projection(x2d, mod, g, w_bf16, segments, seg_dtypes, seq, tm, shift_row, scale_row):
    m, d = x2d.shape
    n = w_bf16.shape[1]
    tiles_per_seq = seq // tm
    kern = functools.partial(_inproj_kernel, segments=tuple(segments),
                             shift_row=shift_row, scale_row=scale_row)
    return pl.pallas_call(
        kern,
        grid=(m // tm,),
        in_specs=[
            pl.BlockSpec((tm, d), lambda i: (i, 0)),
            pl.BlockSpec((1, N_MOD, d), lambda i: (i // tiles_per_seq, 0, 0)),
            pl.BlockSpec((1, d), lambda i: (0, 0)),
            pl.BlockSpec((d, n), lambda i: (0, 0), pipeline_mode=pl.Buffered(1)),
        ],
        out_specs=[pl.BlockSpec((tm, hi - lo), lambda i: (i, 0)) for lo, hi in segments],
        out_shape=[jax.ShapeDtypeStruct((m, hi - lo), dt) for (lo, hi), dt in zip(segments, seg_dtypes)],
        compiler_params=_params("parallel"),
        name="norm_inproj",
    )(x2d, mod, g, w_bf16)


def _dft_stage1_kernel(x_ref, f_ref, o_ref):
    f = f_ref[...].astype(BF16)
    for j in range(x_ref.shape[1]):
        o_ref[j] = _dot(f, x_ref[:, j, :].astype(BF16))


def _dft_stage2_kernel(z_ref, g_ref, cs_ref, o_ref, *, n2):
    cos_t, sin_t = cs_ref[0].astype(BF16), cs_ref[1].astype(BF16)
    for j in range(z_ref.shape[2]):
        zz = jnp.concatenate([z_ref[:, 0, j, :], z_ref[:, 1, j, :]], axis=0).astype(BF16)
        p = _dot(g_ref[j].astype(BF16), zz).astype(BF16)
        o_ref[:, j, :] = _dot(p[:n2], cos_t) + _dot(p[n2:], sin_t)


def _fourier_latent(u, batch, seq):
    c = u.shape[1]
    n1, n2 = DFT_N1, seq // DFT_N1
    f1_np, g_np = _latent_dft_tables(seq)
    f1 = jnp.asarray(f1_np)
    g = jnp.asarray(g_np)
    cs = jnp.asarray(_channel_dft_tables())
    t = DFT_ROWS_PER_STEP
    z = pl.pallas_call(
        _dft_stage1_kernel,
        grid=(batch, n2 // t),
        in_specs=[
            pl.BlockSpec((None, n1, t, c), lambda b, j: (b, 0, j, 0)),
            pl.BlockSpec((2 * n1, n1), lambda b, j: (0, 0)),
        ],
        out_specs=pl.BlockSpec((None, t, 2 * n1, c), lambda b, j: (b, j, 0, 0)),
        out_shape=jax.ShapeDtypeStruct((batch, n2, 2 * n1, c), F32),
        compiler_params=_params("parallel", "parallel"),
        name="dft_stage1",
    )(u.reshape(batch, n1, n2, c), f1)
    y = pl.pallas_call(
        functools.partial(_dft_stage2_kernel, n2=n2),
        grid=(n1 // t, batch),
        in_specs=[
            pl.BlockSpec((None, n2, 2, t, c), lambda k, b: (b, 0, 0, k, 0)),
            pl.BlockSpec((t, 2 * n2, 2 * n2), lambda k, b: (k, 0, 0)),
            pl.BlockSpec((2, c, c), lambda k, b: (0, 0, 0)),
        ],
        out_specs=pl.BlockSpec((None, n2, t, c), lambda k, b: (b, 0, k, 0)),
        out_shape=jax.ShapeDtypeStruct((batch, n2, n1, c), F32),
        compiler_params=_params("parallel", "parallel"),
        name="dft_stage2",
    )(z.reshape(batch, n2, 2, n1, c), g, cs)
    return y.reshape(batch * seq, c)


def _dft_context_kernel(u_ref, f_ref, cs_ref, o_ref, *, n):
    p = _dot(f_ref[...].astype(BF16), u_ref[...].astype(BF16)).astype(BF16)
    y = _dot(p[:n], cs_ref[0].astype(BF16)) + _dot(p[n:], cs_ref[1].astype(BF16))
    o_ref[...] = y.astype(o_ref.dtype)


def _fourier_context(u, batch, seq):
    c = u.shape[1]
    f = jnp.asarray(_context_dft_table(seq))
    cs = jnp.asarray(_channel_dft_tables())
    return pl.pallas_call(
        functools.partial(_dft_context_kernel, n=seq),
        grid=(batch,),
        in_specs=[
            pl.BlockSpec((seq, c), lambda b: (b, 0)),
            pl.BlockSpec((2 * seq, seq), lambda b: (0, 0)),
            pl.BlockSpec((2, c, c), lambda b: (0, 0, 0)),
        ],
        out_specs=pl.BlockSpec((seq, c), lambda b: (b, 0)),
        out_shape=jax.ShapeDtypeStruct((batch * seq, c), BF16),
        compiler_params=_params("parallel"),
        name="dft_context",
    )(u, f, cs)


def _attend_heads(q_ref, key_refs, value_refs, bias_fn, o_ref):
    q_scale = HEAD_DIM ** -0.5 * LOG2E

    def lanes(h):
        return slice((h // 2) * LANES, (h // 2 + 1) * LANES)

    def own_lanes(shape, h):
        low = lax.broadcasted_iota(jnp.int32, shape, 1) < HEAD_DIM
        return low if h % 2 == 0 else jnp.logical_not(low)

    def logits(h):
        q_pair = q_ref[:, lanes(h)]
        qs = (q_pair.astype(F32) * q_scale).astype(BF16)
        qm = jnp.where(own_lanes(qs.shape, h), qs, jnp.zeros_like(qs))
        out = []
        for j, k_ref in enumerate(key_refs):
            sj = _dot_nt(qm, k_ref[:, lanes(h)])
            bj = bias_fn(h, j)
            out.append(sj if bj is None else sj + bj)
        return out

    s_next = logits(0)
    even_out = None
    for h in range(N_NA_HEADS):
        s = s_next
        if h + 1 < N_NA_HEADS:
            s_next = logits(h + 1)
        m = s[0]
        for sj in s[1:]:
            m = jnp.maximum(m, sj)
        m = jnp.max(m, axis=-1, keepdims=True)
        p = jnp.concatenate([jnp.exp2(sj - m).astype(BF16) for sj in s], axis=1)
        v = jnp.concatenate([v_ref[:, lanes(h)] for v_ref in value_refs], axis=0)
        v_aug = jnp.where(own_lanes(v.shape, h), v, jnp.ones_like(v))
        acc = _dot(p, v_aug)
        denom = pltpu.roll(acc, HEAD_DIM, 1)
        out = acc * (1.0 / denom)
        if h % 2 == 0:
            even_out = out
        else:
            o_ref[:, lanes(h)] = jnp.where(own_lanes(out.shape, 0), even_out, out).astype(o_ref.dtype)


def _natten_kernel(q_ref, *refs):
    nb = ATTN_KEY_BLOCKS
    k_refs, v_refs = refs[:nb], refs[nb:2 * nb]
    kc_ref, vc_ref, bias_ref, o_ref = refs[2 * nb:]
    kw = ATTN_Q

    def bias_fn(h, j):
        return bias_ref[h, :, j * kw:(j + 1) * kw] if j < nb else None

    _attend_heads(q_ref, list(k_refs) + [kc_ref], list(v_refs) + [vc_ref], bias_fn, o_ref)


def _neighbourhood_attention(qkv, kv_ctx, ctx_k_col, bias_tables, layer, batch, seq, ctx_len):
    m = qkv.shape[0]
    tq = ATTN_Q
    nrb = seq // tq
    assert ctx_len == tq, "context keys are processed as one key block"

    def q_map(b, r):
        return (b * nrb + r, 0)

    def kv_map(j, col):
        return lambda b, r: (b * nrb + _attn_key_block_start(r, nrb) + j, col)

    def bias_map(b, r):
        return (layer, jnp.where(r == 0, 0, jnp.where(r == nrb - 1, 2, 1)), 0, 0, 0)

    blk = (tq, NA_WIDTH)
    in_specs = [pl.BlockSpec(blk, q_map)]
    in_specs += [pl.BlockSpec(blk, kv_map(j, 1)) for j in range(ATTN_KEY_BLOCKS)]
    in_specs += [pl.BlockSpec(blk, kv_map(j, 2)) for j in range(ATTN_KEY_BLOCKS)]
    in_specs += [
        pl.BlockSpec((ctx_len, NA_WIDTH), lambda b, r: (b, ctx_k_col)),
        pl.BlockSpec((ctx_len, NA_WIDTH), lambda b, r: (b, ctx_k_col + 1)),
        pl.BlockSpec((None, None, N_NA_HEADS, tq, ATTN_KEY_BLOCKS * tq), bias_map),
    ]
    args = [qkv] * (1 + 2 * ATTN_KEY_BLOCKS) + [kv_ctx, kv_ctx, bias_tables]
    return pl.pallas_call(
        _natten_kernel,
        grid=(batch, nrb),
        in_specs=in_specs,
        out_specs=pl.BlockSpec(blk, q_map),
        out_shape=jax.ShapeDtypeStruct((m, NA_WIDTH), BF16),
        compiler_params=_params("parallel", "arbitrary"),
        name="neighbourhood_attention",
    )(*args)


def _ctx_attn_kernel(q_ref, k_ref, v_ref, o_ref):
    _attend_heads(q_ref, [k_ref], [v_ref], lambda h, j: None, o_ref)


def _context_attention(qkv, batch, ctx_len):
    blk = (ctx_len, NA_WIDTH)
    return pl.pallas_call(
        _ctx_attn_kernel,
        grid=(batch,),
        in_specs=[pl.BlockSpec(blk, lambda b: (b, 0)),
                  pl.BlockSpec(blk, lambda b: (b, 1)),
                  pl.BlockSpec(blk, lambda b: (b, 2))],
        out_specs=pl.BlockSpec(blk, lambda b: (b, 0)),
        out_shape=jax.ShapeDtypeStruct((batch * ctx_len, NA_WIDTH), BF16),
        compiler_params=_params("parallel"),
        name="context_attention",
    )(qkv, qkv, qkv)


HALO_ROWS = 16


def _merge_kernel(x_ref, mod_ref, f_ref, cv_ref, cvp_ref, cvn_ref, at_ref, gt_ref, cw_ref,
                  wf_ref, wc_ref, wa_ref, wo_ref, o_ref, cv_scr, *, tiles_per_seq, gate_row):
    i = pl.program_id(0)
    tm = x_ref.shape[0]
    cw = CONV_WIDTH
    has_prev = (i % tiles_per_seq) != 0
    has_next = (i % tiles_per_seq) != tiles_per_seq - 1

    u = cv_ref[:, 0:cw].astype(F32)
    gb = cv_ref[:, cw:2 * cw].astype(F32)
    gc = cv_ref[:, 2 * cw:3 * cw].astype(F32)
    z = gc * u
    zp_row = (cvp_ref[HALO_ROWS - 1:HALO_ROWS, 2 * cw:3 * cw].astype(F32)
              * cvp_ref[HALO_ROWS - 1:HALO_ROWS, 0:cw].astype(F32))
    zn_row = cvn_ref[0:1, 2 * cw:3 * cw].astype(F32) * cvn_ref[0:1, 0:cw].astype(F32)
    zp_row = jnp.where(has_prev, zp_row, 0.0)
    zn_row = jnp.where(has_next, zn_row, 0.0)
    row = lax.broadcasted_iota(jnp.int32, (tm, cw), 0)
    z_prev = jnp.where(row == 0, zp_row, pltpu.roll(z, 1, 0))
    z_next = jnp.where(row == tm - 1, zn_row, pltpu.roll(z, tm - 1, 0))
    y = cw_ref[0:1, :] * z_prev + cw_ref[1:2, :] * z + cw_ref[2:3, :] * z_next
    cv_scr[...] = (gb * y).astype(BF16)

    d = D_MODEL
    gate = mod_ref[0, gate_row:gate_row + 1, :]

    def branches(c):
        r = slice(c * MERGE_ROW_CHUNK, (c + 1) * MERGE_ROW_CHUNK)
        return (_dot(f_ref[r, :].astype(BF16), wf_ref[...]), _dot(cv_scr[r, :], wc_ref[...]),
                _dot(at_ref[r, :], wa_ref[...]))

    nxt = branches(0)
    n_chunks = tm // MERGE_ROW_CHUNK
    for c in range(n_chunks):
        r = slice(c * MERGE_ROW_CHUNK, (c + 1) * MERGE_ROW_CHUNK)
        pf, pc, pa = nxt
        if c + 1 < n_chunks:
            nxt = branches(c + 1)
        mix = _twice_sigmoid(gt_ref[r, 0:d]) * pf
        mix = mix + _twice_sigmoid(gt_ref[r, d:2 * d]) * pc
        mix = mix + _twice_sigmoid(gt_ref[r, 2 * d:3 * d]) * pa
        proj = _dot(mix.astype(BF16), wo_ref[...])
        o_ref[r, :] = x_ref[r, :] + (0.5 * gate) * proj


def _merge(x2d, mod, f, conv, attn, gates, conv_w, w_f, w_c, w_a, w_o, seq, tm, gate_row):
    m, d = x2d.shape
    tiles_per_seq = seq // tm
    halo_per_tile = tm // HALO_ROWS
    n_halo = m // HALO_ROWS
    kern = functools.partial(_merge_kernel, tiles_per_seq=tiles_per_seq, gate_row=gate_row)
    const = lambda i: (0, 0)
    return pl.pallas_call(
        kern,
        grid=(m // tm,),
        in_specs=[
            pl.BlockSpec((tm, d), lambda i: (i, 0)),
            pl.BlockSpec((1, N_MOD, d), lambda i: (i // tiles_per_seq, 0, 0)),
            pl.BlockSpec((tm, FOURIER_WIDTH), lambda i: (i, 0)),
            pl.BlockSpec((tm, 3 * CONV_WIDTH), lambda i: (i, 0)),
            pl.BlockSpec((HALO_ROWS, 3 * CONV_WIDTH),
                         lambda i: (jnp.maximum(i * halo_per_tile - 1, 0), 0)),
            pl.BlockSpec((HALO_ROWS, 3 * CONV_WIDTH),
                         lambda i: (jnp.minimum((i + 1) * halo_per_tile, n_halo - 1), 0)),
            pl.BlockSpec((tm, NA_WIDTH), lambda i: (i, 0)),
            pl.BlockSpec((tm, 3 * d), lambda i: (i, 0)),
            pl.BlockSpec((CONV_K, CONV_WIDTH), const),
            pl.BlockSpec((FOURIER_WIDTH, d), const),
            pl.BlockSpec((CONV_WIDTH, d), const),
            pl.BlockSpec((NA_WIDTH, d), const),
            pl.BlockSpec((d, d), const),
        ],
        out_specs=pl.BlockSpec((tm, d), lambda i: (i, 0)),
        out_shape=jax.ShapeDtypeStruct((m, d), F32),
        scratch_shapes=[pltpu.VMEM((tm, CONV_WIDTH), BF16)],
        compiler_params=_params("parallel"),
        name="merge_outproj",
    )(x2d, mod, f, conv, conv, conv, attn, gates, conv_w, w_f, w_c, w_a, w_o)


def _mlp_kernel(x_ref, mod_ref, g_ref, w1_ref, w2_ref, fg_ref, o_ref, *, final_norm):
    x = x_ref[...]
    h = _norm_modulate(x, g_ref[...], mod_ref[0, 3:4, :], mod_ref[0, 4:5, :]).astype(BF16)
    acc = None
    for c0 in range(0, MLP_HIDDEN, MLP_HIDDEN_CHUNK):
        a = jnp.maximum(_dot(h, w1_ref[:, c0:c0 + MLP_HIDDEN_CHUNK]), 0.0)
        part = _dot((a * a).astype(BF16), w2_ref[c0:c0 + MLP_HIDDEN_CHUNK, :])
        acc = part if acc is None else acc + part
    y = x + mod_ref[0, 5:6, :] * acc
    if final_norm:
        ms = jnp.mean(y * y, axis=-1, keepdims=True)
        y = y * lax.rsqrt(ms + EPS) * fg_ref[...]
    o_ref[...] = y


def _mlp(x2d, mod, g, w1, w2, final_g, seq, tm, final_norm):
    m, d = x2d.shape
    tiles_per_seq = seq // tm
    const = lambda i: (0, 0)
    return pl.pallas_call(
        functools.partial(_mlp_kernel, final_norm=final_norm),
        grid=(m // tm,),
        in_specs=[
            pl.BlockSpec((tm, d), lambda i: (i, 0)),
            pl.BlockSpec((1, N_MOD, d), lambda i: (i // tiles_per_seq, 0, 0)),
            pl.BlockSpec((1, d), const),
            pl.BlockSpec((d, MLP_HIDDEN), const, pipeline_mode=pl.Buffered(1)),
            pl.BlockSpec((MLP_HIDDEN, d), const, pipeline_mode=pl.Buffered(1)),
            pl.BlockSpec((1, d), const),
        ],
        out_specs=pl.BlockSpec((tm, d), lambda i: (i, 0)),
        out_shape=jax.ShapeDtypeStruct((m, d), F32),
        compiler_params=_params("parallel"),
        name="mlp",
    )(x2d, mod, g, w1, w2, final_g)


ALL_SEGMENTS = (SEG_FOURIER, SEG_CONV, SEG_QKV, SEG_GATES)
SEGMENT_DTYPES = (F32, BF16, BF16, BF16)


def kernel(x, c, ctx, c_ctx, ada_w, ada_b, norm1_g, norm2_g, w_in, conv_w, rel_bias,
           w_fourier, w_conv, w_attn, w_o, mlp_w1, mlp_w2, final_g):
    batch, seq, d = x.shape
    ctx_len = ctx.shape[1]
    rows = seq // GRID_W
    depth = ada_w.shape[0]

    n_cond = batch + 1
    pad = (-n_cond) % 8
    cond = jnp.concatenate([c, c_ctx[None, :], jnp.zeros((pad, d), F32)], axis=0)
    mod_all = _modulation(cond, ada_w, ada_b)

    x2 = x.reshape(batch * seq, d)
    c2 = ctx.reshape(batch * ctx_len, d)
    fg = final_g.reshape(1, d)
    bias_tables = _attn_bias_table(rel_bias, rows)

    for l in range(depth):
        last = l == depth - 1
        mod_x = mod_all[l, :batch].reshape(batch, N_MOD, d)
        mod_c = jnp.broadcast_to(mod_all[l, batch].reshape(1, N_MOD, d), (batch, N_MOD, d))
        g1 = norm1_g[l].reshape(1, d)
        g2 = norm2_g[l].reshape(1, d)
        w_in_b = w_in[l].astype(BF16)
        w_f_b = w_fourier[l].astype(BF16)
        w_c_b = w_conv[l].astype(BF16)
        w_a_b = w_attn[l].astype(BF16)
        w_o_b = w_o[l].astype(BF16)
        w1_b = mlp_w1[l].astype(BF16)
        w2_b = mlp_w2[l].astype(BF16)

        if last:
            (kv_c,) = _in_projection(c2, mod_c, g1, w_in_b[:, KV_START:KV_END],
                                     ((0, KV_END - KV_START),), (BF16,), ctx_len, ctx_len, 0, 1)
            ctx_k_col = 0
        else:
            uf_c, conv_c, qkv_c, gates_c = _in_projection(
                c2, mod_c, g1, w_in_b, ALL_SEGMENTS, SEGMENT_DTYPES, ctx_len, ctx_len, 0, 1)
            kv_c, ctx_k_col = qkv_c, 1

        uf_x, conv_x, qkv_x, gates_x = _in_projection(
            x2, mod_x, g1, w_in_b, ALL_SEGMENTS, SEGMENT_DTYPES, seq, TM_LATENT, 0, 1)
        f_x = _fourier_latent(uf_x, batch, seq)
        at_x = _neighbourhood_attention(qkv_x, kv_c, ctx_k_col, bias_tables, l, batch, seq, ctx_len)
        x2 = _merge(x2, mod_x, f_x, conv_x, at_x, gates_x, conv_w[l], w_f_b, w_c_b, w_a_b, w_o_b,
                    seq, TM_LATENT, 2)

        if not last:
            f_c = _fourier_context(uf_c, batch, ctx_len)
            at_c = _context_attention(qkv_c, batch, ctx_len)
            c2 = _merge(c2, mod_c, f_c, conv_c, at_c, gates_c, conv_w[l], w_f_b, w_c_b, w_a_b, w_o_b,
                        ctx_len, ctx_len, 2)

        x2 = _mlp(x2, mod_x, g2, w1_b, w2_b, fg, seq, TM_LATENT, final_norm=last)
        if not last:
            c2 = _mlp(c2, mod_c, g2, w1_b, w2_b, fg, ctx_len, ctx_len, final_norm=False)

    return x2.reshape(batch, seq, d)
```

```python
import functools
import math

import numpy as np
import jax
import jax.numpy as jnp
from jax import lax
from jax.experimental import pallas as pl
from jax.experimental.pallas import tpu as pltpu

D_MODEL = 1024
DEPTH = 2
GRID_W = 64
HEAD_DIM = 64
N_NA_HEADS = 8
NA_WIDTH = N_NA_HEADS * HEAD_DIM
N_FOURIER_GROUPS = 4
FOURIER_WIDTH = D_MODEL // 4
FOURIER_GROUP = FOURIER_WIDTH // N_FOURIER_GROUPS
CONV_WIDTH = D_MODEL // 4
CONV_K = 3
WIN_ROWS = 8
WIN_COLS = 16
MLP_HIDDEN = 4 * D_MODEL
N_MOD = 6
EPS = 1e-6
NEG = -1e30
LOG2E = math.log2(math.e)

SEG_FOURIER = (0, FOURIER_WIDTH)
SEG_CONV = (SEG_FOURIER[1], SEG_FOURIER[1] + 3 * CONV_WIDTH)
SEG_QKV = (SEG_CONV[1], SEG_CONV[1] + 3 * NA_WIDTH)
SEG_GATES = (SEG_QKV[1], SEG_QKV[1] + 3 * D_MODEL)
KV_START = SEG_QKV[0] + NA_WIDTH
KV_END = SEG_QKV[1]

LANES = 128
MXU_DIM = 256
VMEM_LIMIT_BYTES = 56 * 1024 * 1024

TM_LATENT = 512
ROWS_PER_ATTN_STEP = 4
ATTN_Q = ROWS_PER_ATTN_STEP * GRID_W
ATTN_KEY_BLOCKS = 3
DFT_N1 = 64
DFT_ROWS_PER_STEP = 8
MOD_COL_TILE = 1536
PROJ_COL_CHUNK = 512
MLP_HIDDEN_CHUNK = 1024
MERGE_ROW_CHUNK = 256

BF16 = jnp.bfloat16
F32 = jnp.float32


def _params(*semantics):
    return pltpu.CompilerParams(dimension_semantics=semantics, vmem_limit_bytes=VMEM_LIMIT_BYTES)


def _dot(a, b):
    return jnp.dot(a, b, preferred_element_type=F32)


def _dot_nt(a, b):
    return lax.dot_general(a, b, (((1,), (1,)), ((), ())), preferred_element_type=F32)


def _channel_dft_tables():
    c = np.arange(FOURIER_GROUP)
    ang = 2.0 * np.pi * ((c[:, None] * c[None, :]) % FOURIER_GROUP) / FOURIER_GROUP
    eye = np.eye(N_FOURIER_GROUPS)
    s = 1.0 / math.sqrt(FOURIER_GROUP)
    return np.stack([np.kron(eye, np.cos(ang) * s), np.kron(eye, np.sin(ang) * s)]).astype(np.float32)


def _latent_dft_tables(n):
    n1, n2 = DFT_N1, n // DFT_N1
    a = np.arange(n1)
    ang1 = 2.0 * np.pi * ((a[:, None] * a[None, :]) % n1) / n1
    s1 = 1.0 / math.sqrt(n1)
    f1 = np.concatenate([np.cos(ang1) * s1, -np.sin(ang1) * s1], axis=0)
    k1 = np.arange(n1)[:, None, None]
    k2 = np.arange(n2)[None, :, None]
    m2 = np.arange(n2)[None, None, :]
    ang = 2.0 * np.pi * ((m2 * (k1 + n1 * k2)) % n) / n
    s2 = 1.0 / math.sqrt(n2)
    tr, ti = np.cos(ang) * s2, -np.sin(ang) * s2
    g = np.concatenate([np.concatenate([tr, -ti], axis=2),
                        np.concatenate([ti, tr], axis=2)], axis=1)
    return f1.astype(np.float32), g.astype(np.float32)


def _context_dft_table(n):
    a = np.arange(n)
    ang = 2.0 * np.pi * ((a[:, None] * a[None, :]) % n) / n
    s = 1.0 / math.sqrt(n)
    return np.concatenate([np.cos(ang) * s, -np.sin(ang) * s], axis=0).astype(np.float32)


def _attn_key_block_start(rb, n_row_blocks):
    return jnp.clip(rb - 1, 0, n_row_blocks - ATTN_KEY_BLOCKS)


def _attn_bias_table(rel_bias, rows):
    depth, h, ndr, _ = rel_bias.shape
    w = GRID_W
    rq, nkr = ROWS_PER_ATTN_STEP, ROWS_PER_ATTN_STEP * ATTN_KEY_BLOCKS
    ends = w - WIN_COLS
    ext = jnp.concatenate([
        jnp.broadcast_to(rel_bias[..., :1], (depth, h, ndr, ends)), rel_bias,
        jnp.broadcast_to(rel_bias[..., -1:], (depth, h, ndr, ends)), jnp.zeros((depth, h, ndr, 1), F32)], axis=-1)
    return pl.pallas_call(
        functools.partial(_bias_table_kernel, rows=rows),
        grid=(depth, h),
        in_specs=[pl.BlockSpec((None, None, ndr, 2 * w), lambda l, hh: (l, hh, 0, 0))],
        out_specs=pl.BlockSpec((None, 3, None, rq * w, nkr * w), lambda l, hh: (l, 0, hh, 0, 0)),
        out_shape=jax.ShapeDtypeStruct((depth, 3, h, rq * w, nkr * w), F32),
        compiler_params=_params("parallel", "parallel"),
        name="attn_bias_table",
    )(ext)


def _bias_table_kernel(ext_ref, o_ref, *, rows):
    w = GRID_W
    rq, nkr = ROWS_PER_ATTN_STEP, ROWS_PER_ATTN_STEP * ATTN_KEY_BLOCKS
    q = lax.broadcasted_iota(jnp.int32, (w, 2 * w), 0)
    t = lax.broadcasted_iota(jnp.int32, (w, 2 * w), 1)
    first_col = jnp.clip(q - WIN_COLS // 2, 0, w - WIN_COLS)
    neg = jnp.full((w, 2 * w), NEG, F32)
    left, right = [], []
    for d in range(2 * WIN_ROWS - 1):
        row = jnp.broadcast_to(ext_ref[d:d + 1, :] * LOG2E, (w, 2 * w))
        lo = pltpu.roll(row, w + 1, 1, stride=1, stride_axis=0)
        hi = pltpu.roll(row, 1, 1, stride=1, stride_axis=0)
        left.append(jnp.where((t >= first_col) & (t < first_col + WIN_COLS), lo, neg))
        right.append(jnp.where((t - w >= first_col) & (t - w < first_col + WIN_COLS), hi, neg))

    variants = [(0, 0), (rq, 0), (rows - rq, rows - nkr)]
    for v, (r0, ks) in enumerate(variants):
        for qi in range(rq):
            qr = r0 + qi
            rs = min(max(qr - WIN_ROWS // 2, 0), rows - WIN_ROWS)
            lead = rs - ks
            d0 = rs - qr + WIN_ROWS - 1

            def slab(kj, side):
                return side[d0 + kj - lead] if lead <= kj < lead + WIN_ROWS else neg

            for p in range(nkr // 2):
                tile = jnp.where(t < w, slab(2 * p, left), slab(2 * p + 1, right))
                o_ref[v, qi * w:(qi + 1) * w, p * 2 * w:(p + 1) * 2 * w] = tile


def _mod_kernel(c_ref, w_ref, b_ref, o_ref):
    c = c_ref[...]
    s = c * (1.0 / (1.0 + jnp.exp(-c)))
    o_ref[...] = jnp.dot(s, w_ref[...], precision=lax.Precision.HIGHEST,
                         preferred_element_type=F32) + b_ref[...]


def _modulation(cond, ada_w, ada_b):
    r, d = cond.shape
    depth, _, n = ada_w.shape
    tn = MOD_COL_TILE
    return pl.pallas_call(
        _mod_kernel,
        grid=(depth, n // tn),
        in_specs=[
            pl.BlockSpec((r, d), lambda l, j: (0, 0)),
            pl.BlockSpec((None, d, tn), lambda l, j: (l, 0, j)),
            pl.BlockSpec((None, 1, tn), lambda l, j: (l, 0, j)),
        ],
        out_specs=pl.BlockSpec((None, r, tn), lambda l, j: (l, 0, j)),
        out_shape=jax.ShapeDtypeStruct((depth, r, n), F32),
        compiler_params=_params("parallel", "parallel"),
        name="adaln_modulation",
    )(cond, ada_w, ada_b.reshape(depth, 1, n))


def _norm_modulate(x, g, shift, scale):
    ms = jnp.mean(x * x, axis=-1, keepdims=True)
    y = x * lax.rsqrt(ms + EPS) * g
    return y * (1.0 + scale) + shift


def _twice_sigmoid(g_bf16):
    return jnp.tanh((g_bf16 * 0.5).astype(F32)) + 1.0


def _inproj_kernel(x_ref, mod_ref, g_ref, w_ref, *out_refs, segments, shift_row, scale_row):
    h = _norm_modulate(x_ref[...], g_ref[...],
                       mod_ref[0, shift_row:shift_row + 1, :], mod_ref[0, scale_row:scale_row + 1, :])
    hb = h.astype(BF16)
    for o_ref, (lo, hi) in zip(out_refs, segments):
        for c0 in range(lo, hi, PROJ_COL_CHUNK):
            c1 = min(c0 + PROJ_COL_CHUNK, hi)
            o_ref[:, c0 - lo:c1 - lo] = _dot(hb, w_ref[:, c0:c1]).astype(o_ref.dtype)


def _in_projection(x2d, mod, g, w_bf16, segments, seg_dtypes, seq, tm, shift_row, scale_row):
    m, d = x2d.shape
    n = w_bf16.shape[1]
    tiles_per_seq = seq // tm
    kern = functools.partial(_inproj_kernel, segments=tuple(segments),
                             shift_row=shift_row, scale_row=scale_row)
    return pl.pallas_call(
        kern,
        grid=(m // tm,),
        in_specs=[
            pl.BlockSpec((tm, d), lambda i: (i, 0)),
            pl.BlockSpec((1, N_MOD, d), lambda i: (i // tiles_per_seq, 0, 0)),
            pl.BlockSpec((1, d), lambda i: (0, 0)),
            pl.BlockSpec((d, n), lambda i: (0, 0), pipeline_mode=pl.Buffered(1)),
        ],
        out_specs=[pl.BlockSpec((tm, hi - lo), lambda i: (i, 0)) for lo, hi in segments],
        out_shape=[jax.ShapeDtypeStruct((m, hi - lo), dt) for (lo, hi), dt in zip(segments, seg_dtypes)],
        compiler_params=_params("parallel"),
        name="norm_inproj",
    )(x2d, mod, g, w_bf16)


def _dft_stage1_kernel(x_ref, f_ref, o_ref):
    n1, t, c = x_ref.shape
    x = x_ref[...].reshape(n1 * t, c).astype(BF16)
    z = _dot(f_ref[...].astype(BF16), x)
    o_ref[...] = z.reshape(2, n1, t, c)


def _dft_stage2_kernel(z_ref, g_ref, cs_ref, o_ref, *, n2):
    cos_t, sin_t = cs_ref[0].astype(BF16), cs_ref[1].astype(BF16)
    c = z_ref.shape[3]
    for j in range(z_ref.shape[1]):
        zz = z_ref[:, j].reshape(2 * n2, c).astype(BF16)
        p = _dot(g_ref[j].astype(BF16), zz).astype(BF16)
        o_ref[:, j, :] = _dot(p[:n2], cos_t) + _dot(p[n2:], sin_t)


def _fourier_latent(u, batch, seq):
    c = u.shape[1]
    n1, n2 = DFT_N1, seq // DFT_N1
    f1_np, g_np = _latent_dft_tables(seq)
    t = DFT_ROWS_PER_STEP
    f1 = jnp.asarray(np.kron(f1_np, np.eye(t, dtype=np.float32)))
    g = jnp.asarray(g_np)
    cs = jnp.asarray(_channel_dft_tables())
    z = pl.pallas_call(
        _dft_stage1_kernel,
        grid=(batch, n2 // t),
        in_specs=[
            pl.BlockSpec((None, n1, t, c), lambda b, j: (b, 0, j, 0)),
            pl.BlockSpec((2 * n1 * t, n1 * t), lambda b, j: (0, 0)),
        ],
        out_specs=pl.BlockSpec((None, 2, n1, t, c), lambda b, j: (b, 0, 0, j, 0)),
        out_shape=jax.ShapeDtypeStruct((batch, 2, n1, n2, c), F32),
        compiler_params=_params("parallel", "parallel"),
        name="dft_stage1",
    )(u.reshape(batch, n1, n2, c), f1)
    y = pl.pallas_call(
        functools.partial(_dft_stage2_kernel, n2=n2),
        grid=(n1 // t, batch),
        in_specs=[
            pl.BlockSpec((None, 2, t, n2, c), lambda k, b: (b, 0, k, 0, 0)),
            pl.BlockSpec((t, 2 * n2, 2 * n2), lambda k, b: (k, 0, 0)),
            pl.BlockSpec((2, c, c), lambda k, b: (0, 0, 0)),
        ],
        out_specs=pl.BlockSpec((None, n2, t, c), lambda k, b: (b, 0, k, 0)),
        out_shape=jax.ShapeDtypeStruct((batch, n2, n1, c), F32),
        compiler_params=_params("parallel", "parallel"),
        name="dft_stage2",
    )(z, g, cs)
    return y.reshape(batch * seq, c)


def _dft_context_kernel(u_ref, f_ref, cs_ref, o_ref, *, n):
    p = _dot(f_ref[...].astype(BF16), u_ref[...].astype(BF16)).astype(BF16)
    y = _dot(p[:n], cs_ref[0].astype(BF16)) + _dot(p[n:], cs_ref[1].astype(BF16))
    o_ref[...] = y.astype(o_ref.dtype)


def _fourier_context(u, batch, seq):
    c = u.shape[1]
    f = jnp.asarray(_context_dft_table(seq))
    cs = jnp.asarray(_channel_dft_tables())
    return pl.pallas_call(
        functools.partial(_dft_context_kernel, n=seq),
        grid=(batch,),
        in_specs=[
            pl.BlockSpec((seq, c), lambda b: (b, 0)),
            pl.BlockSpec((2 * seq, seq), lambda b: (0, 0)),
            pl.BlockSpec((2, c, c), lambda b: (0, 0, 0)),
        ],
        out_specs=pl.BlockSpec((seq, c), lambda b: (b, 0)),
        out_shape=jax.ShapeDtypeStruct((batch * seq, c), BF16),
        compiler_params=_params("parallel"),
        name="dft_context",
    )(u, f, cs)


def _attend_heads(q_ref, key_refs, value_refs, bias_fn, o_ref):
    q_scale = HEAD_DIM ** -0.5 * LOG2E

    def lanes(h):
        return slice((h // 2) * LANES, (h // 2 + 1) * LANES)

    def own_lanes(shape, h):
        low = lax.broadcasted_iota(jnp.int32, shape, 1) < HEAD_DIM
        return low if h % 2 == 0 else jnp.logical_not(low)

    def logits(h):
        q_pair = q_ref[:, lanes(h)]
        qs = (q_pair.astype(F32) * q_scale).astype(BF16)
        qm = jnp.where(own_lanes(qs.shape, h), qs, jnp.zeros_like(qs))
        out = []
        for j, k_ref in enumerate(key_refs):
            sj = _dot_nt(qm, k_ref[:, lanes(h)])
            bj = bias_fn(h, j)
            out.append(sj if bj is None else sj + bj)
        return out

    s_next = logits(0)
    even_out = None
    for h in range(N_NA_HEADS):
        s = s_next
        if h + 1 < N_NA_HEADS:
            s_next = logits(h + 1)
        m = s[0]
        for sj in s[1:]:
            m = jnp.maximum(m, sj)
        m = jnp.max(m, axis=-1, keepdims=True)
        p = jnp.concatenate([jnp.exp2(sj - m).astype(BF16) for sj in s], axis=1)
        v = jnp.concatenate([v_ref[:, lanes(h)] for v_ref in value_refs], axis=0)
        v_aug = jnp.where(own_lanes(v.shape, h), v, jnp.ones_like(v))
        acc = _dot(p, v_aug)
        denom = pltpu.roll(acc, HEAD_DIM, 1)
        out = acc * (1.0 / denom)
        if h % 2 == 0:
            even_out = out
        else:
            o_ref[:, lanes(h)] = jnp.where(own_lanes(out.shape, 0), even_out, out).astype(o_ref.dtype)


def _natten_kernel(q_ref, *refs):
    nb = ATTN_KEY_BLOCKS
    k_refs, v_refs = refs[:nb], refs[nb:2 * nb]
    kc_ref, vc_ref, bias_ref, o_ref = refs[2 * nb:]
    kw = ATTN_Q

    def bias_fn(h, j):
        return bias_ref[h, :, j * kw:(j + 1) * kw] if j < nb else None

    _attend_heads(q_ref, list(k_refs) + [kc_ref], list(v_refs) + [vc_ref], bias_fn, o_ref)


def _neighbourhood_attention(qkv, kv_ctx, ctx_k_col, bias_tables, layer, batch, seq, ctx_len):
    m = qkv.shape[0]
    tq = ATTN_Q
    nrb = seq // tq
    assert ctx_len == tq, "context keys are processed as one key block"

    def q_map(b, r):
        return (b * nrb + r, 0)

    def kv_map(j, col):
        return lambda b, r: (b * nrb + _attn_key_block_start(r, nrb) + j, col)

    def bias_map(b, r):
        return (layer, jnp.where(r == 0, 0, jnp.where(r == nrb - 1, 2, 1)), 0, 0, 0)

    blk = (tq, NA_WIDTH)
    in_specs = [pl.BlockSpec(blk, q_map)]
    in_specs += [pl.BlockSpec(blk, kv_map(j, 1)) for j in range(ATTN_KEY_BLOCKS)]
    in_specs += [pl.BlockSpec(blk, kv_map(j, 2)) for j in range(ATTN_KEY_BLOCKS)]
    in_specs += [
        pl.BlockSpec((ctx_len, NA_WIDTH), lambda b, r: (b, ctx_k_col)),
        pl.BlockSpec((ctx_len, NA_WIDTH), lambda b, r: (b, ctx_k_col + 1)),
        pl.BlockSpec((None, None, N_NA_HEADS, tq, ATTN_KEY_BLOCKS * tq), bias_map),
    ]
    args = [qkv] * (1 + 2 * ATTN_KEY_BLOCKS) + [kv_ctx, kv_ctx, bias_tables]
    return pl.pallas_call(
        _natten_kernel,
        grid=(batch, nrb),
        in_specs=in_specs,
        out_specs=pl.BlockSpec(blk, q_map),
        out_shape=jax.ShapeDtypeStruct((m, NA_WIDTH), BF16),
        compiler_params=_params("parallel", "arbitrary"),
        name="neighbourhood_attention",
    )(*args)


def _ctx_attn_kernel(q_ref, k_ref, v_ref, o_ref):
    _attend_heads(q_ref, [k_ref], [v_ref], lambda h, j: None, o_ref)


def _context_attention(qkv, batch, ctx_len):
    blk = (ctx_len, NA_WIDTH)
    return pl.pallas_call(
        _ctx_attn_kernel,
        grid=(batch,),
        in_specs=[pl.BlockSpec(blk, lambda b: (b, 0)),
                  pl.BlockSpec(blk, lambda b: (b, 1)),
                  pl.BlockSpec(blk, lambda b: (b, 2))],
        out_specs=pl.BlockSpec(blk, lambda b: (b, 0)),
        out_shape=jax.ShapeDtypeStruct((batch * ctx_len, NA_WIDTH), BF16),
        compiler_params=_params("parallel"),
        name="context_attention",
    )(qkv, qkv, qkv)


HALO_ROWS = 16


def _merge_mlp_kernel(x_ref, mod_ref, f_ref, cv_ref, cvp_ref, cvn_ref, at_ref, gt_ref, cw_ref,
                      wf_ref, wc_ref, wa_ref, wo_ref, g2_ref, w1_ref, w2_ref, fg_ref, o_ref,
                      cv_scr, x1_scr, *, tiles_per_seq, final_norm):
    i = pl.program_id(0)
    tm = x_ref.shape[0]
    cw = CONV_WIDTH
    has_prev = (i % tiles_per_seq) != 0
    has_next = (i % tiles_per_seq) != tiles_per_seq - 1

    u = cv_ref[:, 0:cw].astype(F32)
    gb = cv_ref[:, cw:2 * cw].astype(F32)
    gc = cv_ref[:, 2 * cw:3 * cw].astype(F32)
    z = gc * u
    zp_row = (cvp_ref[HALO_ROWS - 1:HALO_ROWS, 2 * cw:3 * cw].astype(F32)
              * cvp_ref[HALO_ROWS - 1:HALO_ROWS, 0:cw].astype(F32))
    zn_row = cvn_ref[0:1, 2 * cw:3 * cw].astype(F32) * cvn_ref[0:1, 0:cw].astype(F32)
    zp_row = jnp.where(has_prev, zp_row, 0.0)
    zn_row = jnp.where(has_next, zn_row, 0.0)
    row = lax.broadcasted_iota(jnp.int32, (tm, cw), 0)
    z_prev = jnp.where(row == 0, zp_row, pltpu.roll(z, 1, 0))
    z_next = jnp.where(row == tm - 1, zn_row, pltpu.roll(z, tm - 1, 0))
    y = cw_ref[0:1, :] * z_prev + cw_ref[1:2, :] * z + cw_ref[2:3, :] * z_next
    cv_scr[...] = (gb * y).astype(BF16)

    d = D_MODEL
    gate = mod_ref[0, 2:3, :]

    def branches(c):
        r = slice(c * MERGE_ROW_CHUNK, (c + 1) * MERGE_ROW_CHUNK)
        return (_dot(f_ref[r, :].astype(BF16), wf_ref[...]), _dot(cv_scr[r, :], wc_ref[...]),
                _dot(at_ref[r, :], wa_ref[...]))

    nxt = branches(0)
    n_chunks = tm // MERGE_ROW_CHUNK
    for c in range(n_chunks):
        r = slice(c * MERGE_ROW_CHUNK, (c + 1) * MERGE_ROW_CHUNK)
        pf, pc, pa = nxt
        if c + 1 < n_chunks:
            nxt = branches(c + 1)
        mix = _twice_sigmoid(gt_ref[r, 0:d]) * pf
        mix = mix + _twice_sigmoid(gt_ref[r, d:2 * d]) * pc
        mix = mix + _twice_sigmoid(gt_ref[r, 2 * d:3 * d]) * pa
        proj = _dot(mix.astype(BF16), wo_ref[...])
        x1_scr[r, :] = x_ref[r, :] + (0.5 * gate) * proj

    x1 = x1_scr[...]
    h = _norm_modulate(x1, g2_ref[...], mod_ref[0, 3:4, :], mod_ref[0, 4:5, :]).astype(BF16)
    acc = None
    for c0 in range(0, MLP_HIDDEN, MLP_HIDDEN_CHUNK):
        a = jnp.maximum(_dot(h, w1_ref[:, c0:c0 + MLP_HIDDEN_CHUNK]), 0.0)
        part = _dot((a * a).astype(BF16), w2_ref[c0:c0 + MLP_HIDDEN_CHUNK, :])
        acc = part if acc is None else acc + part
    y = x1 + mod_ref[0, 5:6, :] * acc
    if final_norm:
        ms = jnp.mean(y * y, axis=-1, keepdims=True)
        y = y * lax.rsqrt(ms + EPS) * fg_ref[...]
    o_ref[...] = y


def _merge_mlp(x2d, mod, f, conv, attn, gates, conv_w, w_f, w_c, w_a, w_o, g2, w1, w2, final_g,
               seq, tm, final_norm):
    m, d = x2d.shape
    tiles_per_seq = seq // tm
    halo_per_tile = tm // HALO_ROWS
    n_halo = m // HALO_ROWS
    kern = functools.partial(_merge_mlp_kernel, tiles_per_seq=tiles_per_seq, final_norm=final_norm)
    const = lambda i: (0, 0)
    resident = dict(pipeline_mode=pl.Buffered(1))
    return pl.pallas_call(
        kern,
        grid=(m // tm,),
        in_specs=[
            pl.BlockSpec((tm, d), lambda i: (i, 0)),
            pl.BlockSpec((1, N_MOD, d), lambda i: (i // tiles_per_seq, 0, 0)),
            pl.BlockSpec((tm, FOURIER_WIDTH), lambda i: (i, 0)),
            pl.BlockSpec((tm, 3 * CONV_WIDTH), lambda i: (i, 0)),
            pl.BlockSpec((HALO_ROWS, 3 * CONV_WIDTH),
                         lambda i: (jnp.maximum(i * halo_per_tile - 1, 0), 0)),
            pl.BlockSpec((HALO_ROWS, 3 * CONV_WIDTH),
                         lambda i: (jnp.minimum((i + 1) * halo_per_tile, n_halo - 1), 0)),
            pl.BlockSpec((tm, NA_WIDTH), lambda i: (i, 0)),
            pl.BlockSpec((tm, 3 * d), lambda i: (i, 0)),
            pl.BlockSpec((CONV_K, CONV_WIDTH), const),
            pl.BlockSpec((FOURIER_WIDTH, d), const, **resident),
            pl.BlockSpec((CONV_WIDTH, d), const, **resident),
            pl.BlockSpec((NA_WIDTH, d), const, **resident),
            pl.BlockSpec((d, d), const, **resident),
            pl.BlockSpec((1, d), const),
            pl.BlockSpec((d, MLP_HIDDEN), const, **resident),
            pl.BlockSpec((MLP_HIDDEN, d), const, **resident),
            pl.BlockSpec((1, d), const),
        ],
        out_specs=pl.BlockSpec((tm, d), lambda i: (i, 0)),
        out_shape=jax.ShapeDtypeStruct((m, d), F32),
        scratch_shapes=[pltpu.VMEM((tm, CONV_WIDTH), BF16), pltpu.VMEM((tm, d), F32)],
        compiler_params=_params("parallel"),
        name="merge_mlp",
    )(x2d, mod, f, conv, conv, conv, attn, gates, conv_w, w_f, w_c, w_a, w_o, g2, w1, w2, final_g)


ALL_SEGMENTS = (SEG_FOURIER, SEG_CONV, SEG_QKV, SEG_GATES)
SEGMENT_DTYPES = (F32, BF16, BF16, BF16)


def kernel(x, c, ctx, c_ctx, ada_w, ada_b, norm1_g, norm2_g, w_in, conv_w, rel_bias,
           w_fourier, w_conv, w_attn, w_o, mlp_w1, mlp_w2, final_g):
    batch, seq, d = x.shape
    ctx_len = ctx.shape[1]
    rows = seq // GRID_W
    depth = ada_w.shape[0]

    n_cond = batch + 1
    pad = (-n_cond) % 8
    cond = jnp.concatenate([c, c_ctx[None, :], jnp.zeros((pad, d), F32)], axis=0)
    mod_all = _modulation(cond, ada_w, ada_b)

    x2 = x.reshape(batch * seq, d)
    c2 = ctx.reshape(batch * ctx_len, d)
    fg = final_g.reshape(1, d)
    bias_tables = _attn_bias_table(rel_bias, rows)

    for l in range(depth):
        last = l == depth - 1
        mod_x = mod_all[l, :batch].reshape(batch, N_MOD, d)
        mod_c = jnp.broadcast_to(mod_all[l, batch].reshape(1, N_MOD, d), (batch, N_MOD, d))
        g1 = norm1_g[l].reshape(1, d)
        g2 = norm2_g[l].reshape(1, d)
        w_in_b = w_in[l].astype(BF16)
        w_f_b = w_fourier[l].astype(BF16)
        w_c_b = w_conv[l].astype(BF16)
        w_a_b = w_attn[l].astype(BF16)
        w_o_b = w_o[l].astype(BF16)
        w1_b = mlp_w1[l].astype(BF16)
        w2_b = mlp_w2[l].astype(BF16)

        if last:
            (kv_c,) = _in_projection(c2, mod_c, g1, w_in_b[:, KV_START:KV_END],
                                     ((0, KV_END - KV_START),), (BF16,), ctx_len, ctx_len, 0, 1)
            ctx_k_col = 0
        else:
            uf_c, conv_c, qkv_c, gates_c = _in_projection(
                c2, mod_c, g1, w_in_b, ALL_SEGMENTS, SEGMENT_DTYPES, ctx_len, ctx_len, 0, 1)
            kv_c, ctx_k_col = qkv_c, 1

        uf_x, conv_x, qkv_x, gates_x = _in_projection(
            x2, mod_x, g1, w_in_b, ALL_SEGMENTS, SEGMENT_DTYPES, seq, TM_LATENT, 0, 1)
        f_x = _fourier_latent(uf_x, batch, seq)
        at_x = _neighbourhood_attention(qkv_x, kv_c, ctx_k_col, bias_tables, l, batch, seq, ctx_len)
        x2 = _merge_mlp(x2, mod_x, f_x, conv_x, at_x, gates_x, conv_w[l], w_f_b, w_c_b, w_a_b, w_o_b,
                        g2, w1_b, w2_b, fg, seq, TM_LATENT, final_norm=last)
        if not last:
            f_c = _fourier_context(uf_c, batch, ctx_len)
            at_c = _context_attention(qkv_c, batch, ctx_len)
            c2 = _merge_mlp(c2, mod_c, f_c, conv_c, at_c, gates_c, conv_w[l], w_f_b, w_c_b, w_a_b, w_o_b,
                            g2, w1_b, w2_b, fg, ctx_len, ctx_len, final_norm=False)

    return x2.reshape(batch, seq, d)
```

```python
import functools
import math

import numpy as np
import jax
import jax.numpy as jnp
from jax import lax
from jax.experimental import pallas as pl
from jax.experimental.pallas import tpu as pltpu

D_MODEL = 1024
DEPTH = 2
GRID_W = 64
HEAD_DIM = 64
N_NA_HEADS = 8
NA_WIDTH = N_NA_HEADS * HEAD_DIM
N_FOURIER_GROUPS = 4
FOURIER_WIDTH = D_MODEL // 4
FOURIER_GROUP = FOURIER_WIDTH // N_FOURIER_GROUPS
CONV_WIDTH = D_MODEL // 4
CONV_K = 3
WIN_ROWS = 8
WIN_COLS = 16
MLP_HIDDEN = 4 * D_MODEL
N_MOD = 6
EPS = 1e-6
NEG = -1e30
LOG2E = math.log2(math.e)

SEG_FOURIER = (0, FOURIER_WIDTH)
SEG_CONV = (SEG_FOURIER[1], SEG_FOURIER[1] + 3 * CONV_WIDTH)
SEG_QKV = (SEG_CONV[1], SEG_CONV[1] + 3 * NA_WIDTH)
SEG_GATES = (SEG_QKV[1], SEG_QKV[1] + 3 * D_MODEL)
KV_START = SEG_QKV[0] + NA_WIDTH
KV_END = SEG_QKV[1]

LANES = 128
MXU_DIM = 256
VMEM_LIMIT_BYTES = 56 * 1024 * 1024

TM_LATENT = 512
ROWS_PER_ATTN_STEP = 4
ATTN_Q = ROWS_PER_ATTN_STEP * GRID_W
ATTN_KEY_BLOCKS = 3
ATTN_LOGITS_AHEAD = 1
ATTN_BLOCKS_PER_STEP = 2
DFT_N1 = 64
DFT_ROWS_PER_GROUP = 8
DFT_STAGE1_GROUPS = 4
DFT_STAGE2_GROUPS = 2
MOD_COL_TILE = 1536
PROJ_COL_CHUNK = 512
MLP_HIDDEN_CHUNK = 1024
MERGE_ROW_CHUNK = 256

BF16 = jnp.bfloat16
F32 = jnp.float32


def _params(*semantics):
    return pltpu.CompilerParams(dimension_semantics=semantics, vmem_limit_bytes=VMEM_LIMIT_BYTES)


def _dot(a, b):
    return jnp.dot(a, b, preferred_element_type=F32)


def _dot_nt(a, b):
    return lax.dot_general(a, b, (((1,), (1,)), ((), ())), preferred_element_type=F32)


def _channel_dft_tables():
    c = np.arange(FOURIER_GROUP)
    ang = 2.0 * np.pi * ((c[:, None] * c[None, :]) % FOURIER_GROUP) / FOURIER_GROUP
    eye = np.eye(N_FOURIER_GROUPS)
    s = 1.0 / math.sqrt(FOURIER_GROUP)
    return np.stack([np.kron(eye, np.cos(ang) * s), np.kron(eye, np.sin(ang) * s)]).astype(np.float32)


def _latent_dft_tables(n):
    n1, n2 = DFT_N1, n // DFT_N1
    a = np.arange(n1)
    ang1 = 2.0 * np.pi * ((a[:, None] * a[None, :]) % n1) / n1
    s1 = 1.0 / math.sqrt(n1)
    f1 = np.concatenate([np.cos(ang1) * s1, -np.sin(ang1) * s1], axis=0)
    k1 = np.arange(n1)[:, None, None]
    k2 = np.arange(n2)[None, :, None]
    m2 = np.arange(n2)[None, None, :]
    ang = 2.0 * np.pi * ((m2 * (k1 + n1 * k2)) % n) / n
    s2 = 1.0 / math.sqrt(n2)
    tr, ti = np.cos(ang) * s2, -np.sin(ang) * s2
    g = np.concatenate([np.concatenate([tr, -ti], axis=2),
                        np.concatenate([ti, tr], axis=2)], axis=1)
    return f1.astype(np.float32), g.astype(np.float32)


def _context_dft_table(n):
    a = np.arange(n)
    ang = 2.0 * np.pi * ((a[:, None] * a[None, :]) % n) / n
    s = 1.0 / math.sqrt(n)
    return np.concatenate([np.cos(ang) * s, -np.sin(ang) * s], axis=0).astype(np.float32)


def _attn_key_block_start(rb, n_row_blocks):
    return jnp.clip(rb - 1, 0, n_row_blocks - ATTN_KEY_BLOCKS)


def _attn_bias_table(rel_bias, rows):
    depth, h, ndr, _ = rel_bias.shape
    w = GRID_W
    rq, nkr = ROWS_PER_ATTN_STEP, ROWS_PER_ATTN_STEP * ATTN_KEY_BLOCKS
    ends = w - WIN_COLS
    ext = jnp.concatenate([
        jnp.broadcast_to(rel_bias[..., :1], (depth, h, ndr, ends)), rel_bias,
        jnp.broadcast_to(rel_bias[..., -1:], (depth, h, ndr, ends)), jnp.zeros((depth, h, ndr, 1), F32)], axis=-1)
    return pl.pallas_call(
        functools.partial(_bias_table_kernel, rows=rows),
        grid=(depth, h),
        in_specs=[pl.BlockSpec((None, None, ndr, 2 * w), lambda l, hh: (l, hh, 0, 0))],
        out_specs=pl.BlockSpec((None, 3, None, rq * w, nkr * w), lambda l, hh: (l, 0, hh, 0, 0)),
        out_shape=jax.ShapeDtypeStruct((depth, 3, h, rq * w, nkr * w), F32),
        compiler_params=_params("parallel", "parallel"),
        name="attn_bias_table",
    )(ext)


def _bias_table_kernel(ext_ref, o_ref, *, rows):
    w = GRID_W
    rq, nkr = ROWS_PER_ATTN_STEP, ROWS_PER_ATTN_STEP * ATTN_KEY_BLOCKS
    q = lax.broadcasted_iota(jnp.int32, (w, 2 * w), 0)
    t = lax.broadcasted_iota(jnp.int32, (w, 2 * w), 1)
    first_col = jnp.clip(q - WIN_COLS // 2, 0, w - WIN_COLS)
    neg = jnp.full((w, 2 * w), NEG, F32)
    left, right = [], []
    for d in range(2 * WIN_ROWS - 1):
        row = jnp.broadcast_to(ext_ref[d:d + 1, :] * LOG2E, (w, 2 * w))
        lo = pltpu.roll(row, w + 1, 1, stride=1, stride_axis=0)
        hi = pltpu.roll(row, 1, 1, stride=1, stride_axis=0)
        left.append(jnp.where((t >= first_col) & (t < first_col + WIN_COLS), lo, neg))
        right.append(jnp.where((t - w >= first_col) & (t - w < first_col + WIN_COLS), hi, neg))

    variants = [(0, 0), (rq, 0), (rows - rq, rows - nkr)]
    for v, (r0, ks) in enumerate(variants):
        for qi in range(rq):
            qr = r0 + qi
            rs = min(max(qr - WIN_ROWS // 2, 0), rows - WIN_ROWS)
            lead = rs - ks
            d0 = rs - qr + WIN_ROWS - 1

            def slab(kj, side):
                return side[d0 + kj - lead] if lead <= kj < lead + WIN_ROWS else neg

            for p in range(nkr // 2):
                tile = jnp.where(t < w, slab(2 * p, left), slab(2 * p + 1, right))
                o_ref[v, qi * w:(qi + 1) * w, p * 2 * w:(p + 1) * 2 * w] = tile


def _mod_kernel(c_ref, w_ref, b_ref, o_ref):
    c = c_ref[...]
    s = c * (1.0 / (1.0 + jnp.exp(-c)))
    o_ref[...] = jnp.dot(s, w_ref[...], precision=lax.Precision.HIGHEST,
                         preferred_element_type=F32) + b_ref[...]


def _modulation(cond, ada_w, ada_b):
    r, d = cond.shape
    depth, _, n = ada_w.shape
    tn = MOD_COL_TILE
    return pl.pallas_call(
        _mod_kernel,
        grid=(depth, n // tn),
        in_specs=[
            pl.BlockSpec((r, d), lambda l, j: (0, 0)),
            pl.BlockSpec((None, d, tn), lambda l, j: (l, 0, j)),
            pl.BlockSpec((None, 1, tn), lambda l, j: (l, 0, j)),
        ],
        out_specs=pl.BlockSpec((None, r, tn), lambda l, j: (l, 0, j)),
        out_shape=jax.ShapeDtypeStruct((depth, r, n), F32),
        compiler_params=_params("parallel", "parallel"),
        name="adaln_modulation",
    )(cond, ada_w, ada_b.reshape(depth, 1, n))


def _norm_modulate(x, g, shift, scale):
    ms = jnp.mean(x * x, axis=-1, keepdims=True)
    y = x * lax.rsqrt(ms + EPS) * g
    return y * (1.0 + scale) + shift


def _twice_sigmoid(g_bf16):
    return jnp.tanh((g_bf16 * 0.5).astype(F32)) + 1.0


def _inproj_kernel(x_ref, mod_ref, g_ref, w_ref, *out_refs, segments, shift_row, scale_row):
    h = _norm_modulate(x_ref[...], g_ref[...],
                       mod_ref[0, shift_row:shift_row + 1, :], mod_ref[0, scale_row:scale_row + 1, :])
    hb = h.astype(BF16)
    for o_ref, (lo, hi) in zip(out_refs, segments):
        for c0 in range(lo, hi, PROJ_COL_CHUNK):
            c1 = min(c0 + PROJ_COL_CHUNK, hi)
            o_ref[:, c0 - lo:c1 - lo] = _dot(hb, w_ref[:, c0:c1]).astype(o_ref.dtype)


def _in_projection(x2d, mod, g, w_bf16, segments, seg_dtypes, seq, tm, shift_row, scale_row):
    m, d = x2d.shape
    n = w_bf16.shape[1]
    tiles_per_seq = seq // tm
    kern = functools.partial(_inproj_kernel, segments=tuple(segments),
                             shift_row=shift_row, scale_row=scale_row)
    return pl.pallas_call(
        kern,
        grid=(m // tm,),
        in_specs=[
            pl.BlockSpec((tm, d), lambda i: (i, 0)),
            pl.BlockSpec((1, N_MOD, d), lambda i: (i // tiles_per_seq, 0, 0)),
            pl.BlockSpec((1, d), lambda i: (0, 0)),
            pl.BlockSpec((d, n), lambda i: (0, 0), pipeline_mode=pl.Buffered(1)),
        ],
        out_specs=[pl.BlockSpec((tm, hi - lo), lambda i: (i, 0)) for lo, hi in segments],
        out_shape=[jax.ShapeDtypeStruct((m, hi - lo), dt) for (lo, hi), dt in zip(segments, seg_dtypes)],
        compiler_params=_params("parallel"),
        name="norm_inproj",
    )(x2d, mod, g, w_bf16)


def _dft_stage1_kernel(x_ref, f_ref, o_ref):
    n1, rows, c = x_ref.shape
    t = DFT_ROWS_PER_GROUP
    f = f_ref[...].astype(BF16)
    for g in range(rows // t):
        x = x_ref[:, g * t:(g + 1) * t, :].reshape(n1 * t, c).astype(BF16)
        o_ref[:, :, g * t:(g + 1) * t, :] = _dot(f, x).reshape(2, n1, t, c)


def _dft_stage2_kernel(z_ref, g_ref, cs_ref, o_ref, *, n2):
    cos_t, sin_t = cs_ref[0].astype(BF16), cs_ref[1].astype(BF16)
    c = z_ref.shape[3]
    for j in range(z_ref.shape[1]):
        zz = z_ref[:, j].reshape(2 * n2, c).astype(BF16)
        p = _dot(g_ref[j].astype(BF16), zz).astype(BF16)
        o_ref[:, j, :] = _dot(p[:n2], cos_t) + _dot(p[n2:], sin_t)


def _fourier_latent(u, batch, seq):
    c = u.shape[1]
    n1, n2 = DFT_N1, seq // DFT_N1
    f1_np, g_np = _latent_dft_tables(seq)
    t = DFT_ROWS_PER_GROUP
    r1, r2 = t * DFT_STAGE1_GROUPS, t * DFT_STAGE2_GROUPS
    f1 = jnp.asarray(np.kron(f1_np, np.eye(t, dtype=np.float32)))
    g = jnp.asarray(g_np)
    cs = jnp.asarray(_channel_dft_tables())
    z = pl.pallas_call(
        _dft_stage1_kernel,
        grid=(batch, n2 // r1),
        in_specs=[
            pl.BlockSpec((None, n1, r1, c), lambda b, j: (b, 0, j, 0)),
            pl.BlockSpec((2 * n1 * t, n1 * t), lambda b, j: (0, 0)),
        ],
        out_specs=pl.BlockSpec((None, 2, n1, r1, c), lambda b, j: (b, 0, 0, j, 0)),
        out_shape=jax.ShapeDtypeStruct((batch, 2, n1, n2, c), F32),
        compiler_params=_params("parallel", "parallel"),
        name="dft_stage1",
    )(u.reshape(batch, n1, n2, c), f1)
    y = pl.pallas_call(
        functools.partial(_dft_stage2_kernel, n2=n2),
        grid=(n1 // r2, batch),
        in_specs=[
            pl.BlockSpec((None, 2, r2, n2, c), lambda k, b: (b, 0, k, 0, 0)),
            pl.BlockSpec((r2, 2 * n2, 2 * n2), lambda k, b: (k, 0, 0)),
            pl.BlockSpec((2, c, c), lambda k, b: (0, 0, 0)),
        ],
        out_specs=pl.BlockSpec((None, n2, r2, c), lambda k, b: (b, 0, k, 0)),
        out_shape=jax.ShapeDtypeStruct((batch, n2, n1, c), F32),
        compiler_params=_params("parallel", "parallel"),
        name="dft_stage2",
    )(z, g, cs)
    return y.reshape(batch * seq, c)


def _dft_context_kernel(u_ref, f_ref, cs_ref, o_ref, *, n):
    p = _dot(f_ref[...].astype(BF16), u_ref[...].astype(BF16)).astype(BF16)
    y = _dot(p[:n], cs_ref[0].astype(BF16)) + _dot(p[n:], cs_ref[1].astype(BF16))
    o_ref[...] = y.astype(o_ref.dtype)


def _fourier_context(u, batch, seq):
    c = u.shape[1]
    f = jnp.asarray(_context_dft_table(seq))
    cs = jnp.asarray(_channel_dft_tables())
    return pl.pallas_call(
        functools.partial(_dft_context_kernel, n=seq),
        grid=(batch,),
        in_specs=[
            pl.BlockSpec((seq, c), lambda b: (b, 0)),
            pl.BlockSpec((2 * seq, seq), lambda b: (0, 0)),
            pl.BlockSpec((2, c, c), lambda b: (0, 0, 0)),
        ],
        out_specs=pl.BlockSpec((seq, c), lambda b: (b, 0)),
        out_shape=jax.ShapeDtypeStruct((batch * seq, c), BF16),
        compiler_params=_params("parallel"),
        name="dft_context",
    )(u, f, cs)


def _attend_heads(problems):
    q_scale = HEAD_DIM ** -0.5 * LOG2E

    def lanes(h):
        return slice((h // 2) * LANES, (h // 2 + 1) * LANES)

    def own_lanes(shape, h):
        low = lax.broadcasted_iota(jnp.int32, shape, 1) < HEAD_DIM
        return low if h % 2 == 0 else jnp.logical_not(low)

    def logits(unit):
        h, g = unit
        q_ref, key_refs, _, bias_fn, _ = problems[g]
        q_pair = q_ref[:, lanes(h)]
        qs = (q_pair.astype(F32) * q_scale).astype(BF16)
        qm = jnp.where(own_lanes(qs.shape, h), qs, jnp.zeros_like(qs))
        out = []
        for j, k_ref in enumerate(key_refs):
            sj = _dot_nt(qm, k_ref[:, lanes(h)])
            bj = bias_fn(h, j)
            out.append(sj if bj is None else sj + bj)
        return out

    units = [(h, g) for h in range(N_NA_HEADS) for g in range(len(problems))]
    ahead = ATTN_LOGITS_AHEAD
    queue = [logits(u) for u in units[:ahead]]
    even_out = [None] * len(problems)
    for n, (h, g) in enumerate(units):
        s = queue.pop(0)
        if n + ahead < len(units):
            queue.append(logits(units[n + ahead]))
        _, _, value_refs, _, o_ref = problems[g]
        m = s[0]
        for sj in s[1:]:
            m = jnp.maximum(m, sj)
        m = jnp.max(m, axis=-1, keepdims=True)
        p = jnp.concatenate([jnp.exp2(sj - m).astype(BF16) for sj in s], axis=1)
        v = jnp.concatenate([v_ref[:, lanes(h)] for v_ref in value_refs], axis=0)
        v_aug = jnp.where(own_lanes(v.shape, h), v, jnp.ones_like(v))
        acc = _dot(p, v_aug)
        denom = pltpu.roll(acc, HEAD_DIM, 1)
        out = acc * (1.0 / denom)
        if h % 2 == 0:
            even_out[g] = out
        else:
            o_ref[:, lanes(h)] = jnp.where(own_lanes(out.shape, 0), even_out[g], out).astype(o_ref.dtype)


def _natten_kernel(q_ref, *refs):
    nb, ng, tq = ATTN_KEY_BLOCKS, ATTN_BLOCKS_PER_STEP, ATTN_Q
    kv_refs, (kc_ref, vc_ref), bias_refs, o_ref = (
        refs[:2 * nb * ng], refs[2 * nb * ng:2 * nb * ng + 2], refs[2 * nb * ng + 2:-1], refs[-1])

    def problem(g):
        k_refs = kv_refs[2 * nb * g:2 * nb * g + nb]
        v_refs = kv_refs[2 * nb * g + nb:2 * nb * (g + 1)]
        bias_ref = bias_refs[g]

        def bias_fn(h, j):
            return bias_ref[h, :, j * tq:(j + 1) * tq] if j < nb else None

        rows = pl.ds(g * tq, tq)
        return (q_ref.at[rows], list(k_refs) + [kc_ref], list(v_refs) + [vc_ref], bias_fn, o_ref.at[rows])

    _attend_heads([problem(g) for g in range(ng)])


def _neighbourhood_attention(qkv, kv_ctx, ctx_k_col, bias_tables, layer, batch, seq, ctx_len):
    m = qkv.shape[0]
    tq, ng = ATTN_Q, ATTN_BLOCKS_PER_STEP
    nrb = seq // tq
    assert ctx_len == tq, "context keys are processed as one key block"

    def q_map(b, r):
        return (b * (nrb // ng) + r, 0)

    def kv_map(g, j, col):
        return lambda b, r: (b * nrb + _attn_key_block_start(ng * r + g, nrb) + j, col)

    def bias_map(g):
        def index(b, r):
            rb = ng * r + g
            return (layer, jnp.where(rb == 0, 0, jnp.where(rb == nrb - 1, 2, 1)), 0, 0, 0)
        return index

    blk = (tq, NA_WIDTH)
    in_specs = [pl.BlockSpec((ng * tq, NA_WIDTH), q_map)]
    for g in range(ng):
        in_specs += [pl.BlockSpec(blk, kv_map(g, j, 1)) for j in range(ATTN_KEY_BLOCKS)]
        in_specs += [pl.BlockSpec(blk, kv_map(g, j, 2)) for j in range(ATTN_KEY_BLOCKS)]
    in_specs += [
        pl.BlockSpec((ctx_len, NA_WIDTH), lambda b, r: (b, ctx_k_col)),
        pl.BlockSpec((ctx_len, NA_WIDTH), lambda b, r: (b, ctx_k_col + 1)),
    ]
    in_specs += [pl.BlockSpec((None, None, N_NA_HEADS, tq, ATTN_KEY_BLOCKS * tq), bias_map(g),
                              pipeline_mode=pl.Buffered(1)) for g in range(ng)]
    args = [qkv] * (1 + 2 * ATTN_KEY_BLOCKS * ng) + [kv_ctx, kv_ctx] + [bias_tables] * ng
    return pl.pallas_call(
        _natten_kernel,
        grid=(batch, nrb // ng),
        in_specs=in_specs,
        out_specs=pl.BlockSpec((ng * tq, NA_WIDTH), q_map),
        out_shape=jax.ShapeDtypeStruct((m, NA_WIDTH), BF16),
        compiler_params=_params("parallel", "arbitrary"),
        name="neighbourhood_attention",
    )(*args)


def _ctx_attn_kernel(q_ref, k_ref, v_ref, o_ref):
    _attend_heads([(q_ref, [k_ref], [v_ref], lambda h, j: None, o_ref)])


def _context_attention(qkv, batch, ctx_len):
    blk = (ctx_len, NA_WIDTH)
    return pl.pallas_call(
        _ctx_attn_kernel,
        grid=(batch,),
        in_specs=[pl.BlockSpec(blk, lambda b: (b, 0)),
                  pl.BlockSpec(blk, lambda b: (b, 1)),
                  pl.BlockSpec(blk, lambda b: (b, 2))],
        out_specs=pl.BlockSpec(blk, lambda b: (b, 0)),
        out_shape=jax.ShapeDtypeStruct((batch * ctx_len, NA_WIDTH), BF16),
        compiler_params=_params("parallel"),
        name="context_attention",
    )(qkv, qkv, qkv)


HALO_ROWS = 16


def _merge_mlp_kernel(x_ref, mod_ref, f_ref, cv_ref, cvp_ref, cvn_ref, at_ref, gt_ref, cw_ref,
                      wf_ref, wc_ref, wa_ref, wo_ref, g2_ref, w1_ref, w2_ref, fg_ref, o_ref,
                      cv_scr, x1_scr, *, tiles_per_seq, final_norm):
    i = pl.program_id(0)
    tm = x_ref.shape[0]
    cw = CONV_WIDTH
    has_prev = (i % tiles_per_seq) != 0
    has_next = (i % tiles_per_seq) != tiles_per_seq - 1

    u = cv_ref[:, 0:cw].astype(F32)
    gb = cv_ref[:, cw:2 * cw].astype(F32)
    gc = cv_ref[:, 2 * cw:3 * cw].astype(F32)
    z = gc * u
    zp_row = (cvp_ref[HALO_ROWS - 1:HALO_ROWS, 2 * cw:3 * cw].astype(F32)
              * cvp_ref[HALO_ROWS - 1:HALO_ROWS, 0:cw].astype(F32))
    zn_row = cvn_ref[0:1, 2 * cw:3 * cw].astype(F32) * cvn_ref[0:1, 0:cw].astype(F32)
    zp_row = jnp.where(has_prev, zp_row, 0.0)
    zn_row = jnp.where(has_next, zn_row, 0.0)
    row = lax.broadcasted_iota(jnp.int32, (tm, cw), 0)
    z_prev = jnp.where(row == 0, zp_row, pltpu.roll(z, 1, 0))
    z_next = jnp.where(row == tm - 1, zn_row, pltpu.roll(z, tm - 1, 0))
    y = cw_ref[0:1, :] * z_prev + cw_ref[1:2, :] * z + cw_ref[2:3, :] * z_next
    cv_scr[...] = (gb * y).astype(BF16)

    d = D_MODEL
    gate = mod_ref[0, 2:3, :]

    def branches(c):
        r = slice(c * MERGE_ROW_CHUNK, (c + 1) * MERGE_ROW_CHUNK)
        return (_dot(f_ref[r, :].astype(BF16), wf_ref[...]), _dot(cv_scr[r, :], wc_ref[...]),
                _dot(at_ref[r, :], wa_ref[...]))

    nxt = branches(0)
    n_chunks = tm // MERGE_ROW_CHUNK
    for c in range(n_chunks):
        r = slice(c * MERGE_ROW_CHUNK, (c + 1) * MERGE_ROW_CHUNK)
        pf, pc, pa = nxt
        if c + 1 < n_chunks:
            nxt = branches(c + 1)
        mix = _twice_sigmoid(gt_ref[r, 0:d]) * pf
        mix = mix + _twice_sigmoid(gt_ref[r, d:2 * d]) * pc
        mix = mix + _twice_sigmoid(gt_ref[r, 2 * d:3 * d]) * pa
        proj = _dot(mix.astype(BF16), wo_ref[...])
        x1_scr[r, :] = x_ref[r, :] + (0.5 * gate) * proj

    x1 = x1_scr[...]
    h = _norm_modulate(x1, g2_ref[...], mod_ref[0, 3:4, :], mod_ref[0, 4:5, :]).astype(BF16)
    acc = None
    for c0 in range(0, MLP_HIDDEN, MLP_HIDDEN_CHUNK):
        a = jnp.maximum(_dot(h, w1_ref[:, c0:c0 + MLP_HIDDEN_CHUNK]), 0.0)
        part = _dot((a * a).astype(BF16), w2_ref[c0:c0 + MLP_HIDDEN_CHUNK, :])
        acc = part if acc is None else acc + part
    y = x1 + mod_ref[0, 5:6, :] * acc
    if final_norm:
        ms = jnp.mean(y * y, axis=-1, keepdims=True)
        y = y * lax.rsqrt(ms + EPS) * fg_ref[...]
    o_ref[...] = y


def _merge_mlp(x2d, mod, f, conv, attn, gates, conv_w, w_f, w_c, w_a, w_o, g2, w1, w2, final_g,
               seq, tm, final_norm):
    m, d = x2d.shape
    tiles_per_seq = seq // tm
    halo_per_tile = tm // HALO_ROWS
    n_halo = m // HALO_ROWS
    kern = functools.partial(_merge_mlp_kernel, tiles_per_seq=tiles_per_seq, final_norm=final_norm)
    const = lambda i: (0, 0)
    resident = dict(pipeline_mode=pl.Buffered(1))
    return pl.pallas_call(
        kern,
        grid=(m // tm,),
        in_specs=[
            pl.BlockSpec((tm, d), lambda i: (i, 0)),
            pl.BlockSpec((1, N_MOD, d), lambda i: (i // tiles_per_seq, 0, 0)),
            pl.BlockSpec((tm, FOURIER_WIDTH), lambda i: (i, 0)),
            pl.BlockSpec((tm, 3 * CONV_WIDTH), lambda i: (i, 0)),
            pl.BlockSpec((HALO_ROWS, 3 * CONV_WIDTH),
                         lambda i: (jnp.maximum(i * halo_per_tile - 1, 0), 0)),
            pl.BlockSpec((HALO_ROWS, 3 * CONV_WIDTH),
                         lambda i: (jnp.minimum((i + 1) * halo_per_tile, n_halo - 1), 0)),
            pl.BlockSpec((tm, NA_WIDTH), lambda i: (i, 0)),
            pl.BlockSpec((tm, 3 * d), lambda i: (i, 0)),
            pl.BlockSpec((CONV_K, CONV_WIDTH), const),
            pl.BlockSpec((FOURIER_WIDTH, d), const, **resident),
            pl.BlockSpec((CONV_WIDTH, d), const, **resident),
            pl.BlockSpec((NA_WIDTH, d), const, **resident),
            pl.BlockSpec((d, d), const, **resident),
            pl.BlockSpec((1, d), const),
            pl.BlockSpec((d, MLP_HIDDEN), const, **resident),
            pl.BlockSpec((MLP_HIDDEN, d), const, **resident),
            pl.BlockSpec((1, d), const),
        ],
        out_specs=pl.BlockSpec((tm, d), lambda i: (i, 0)),
        out_shape=jax.ShapeDtypeStruct((m, d), F32),
        scratch_shapes=[pltpu.VMEM((tm, CONV_WIDTH), BF16), pltpu.VMEM((tm, d), F32)],
        compiler_params=_params("parallel"),
        name="merge_mlp",
    )(x2d, mod, f, conv, conv, conv, attn, gates, conv_w, w_f, w_c, w_a, w_o, g2, w1, w2, final_g)


ALL_SEGMENTS = (SEG_FOURIER, SEG_CONV, SEG_QKV, SEG_GATES)
SEGMENT_DTYPES = (F32, BF16, BF16, BF16)


def kernel(x, c, ctx, c_ctx, ada_w, ada_b, norm1_g, norm2_g, w_in, conv_w, rel_bias,
           w_fourier, w_conv, w_attn, w_o, mlp_w1, mlp_w2, final_g):
    batch, seq, d = x.shape
    ctx_len = ctx.shape[1]
    rows = seq // GRID_W
    depth = ada_w.shape[0]

    n_cond = batch + 1
    pad = (-n_cond) % 8
    cond = jnp.concatenate([c, c_ctx[None, :], jnp.zeros((pad, d), F32)], axis=0)
    mod_all = _modulation(cond, ada_w, ada_b)

    x2 = x.reshape(batch * seq, d)
    c2 = ctx.reshape(batch * ctx_len, d)
    fg = final_g.reshape(1, d)
    bias_tables = _attn_bias_table(rel_bias, rows)

    for l in range(depth):
        last = l == depth - 1
        mod_x = mod_all[l, :batch].reshape(batch, N_MOD, d)
        mod_c = jnp.broadcast_to(mod_all[l, batch].reshape(1, N_MOD, d), (batch, N_MOD, d))
        g1 = norm1_g[l].reshape(1, d)
        g2 = norm2_g[l].reshape(1, d)
        w_in_b = w_in[l].astype(BF16)
        w_f_b = w_fourier[l].astype(BF16)
        w_c_b = w_conv[l].astype(BF16)
        w_a_b = w_attn[l].astype(BF16)
        w_o_b = w_o[l].astype(BF16)
        w1_b = mlp_w1[l].astype(BF16)
        w2_b = mlp_w2[l].astype(BF16)

        if last:
            (kv_c,) = _in_projection(c2, mod_c, g1, w_in_b[:, KV_START:KV_END],
                                     ((0, KV_END - KV_START),), (BF16,), ctx_len, ctx_len, 0, 1)
            ctx_k_col = 0
        else:
            uf_c, conv_c, qkv_c, gates_c = _in_projection(
                c2, mod_c, g1, w_in_b, ALL_SEGMENTS, SEGMENT_DTYPES, ctx_len, ctx_len, 0, 1)
            kv_c, ctx_k_col = qkv_c, 1

        uf_x, conv_x, qkv_x, gates_x = _in_projection(
            x2, mod_x, g1, w_in_b, ALL_SEGMENTS, SEGMENT_DTYPES, seq, TM_LATENT, 0, 1)
        f_x = _fourier_latent(uf_x, batch, seq)
        at_x = _neighbourhood_attention(qkv_x, kv_c, ctx_k_col, bias_tables, l, batch, seq, ctx_len)
        x2 = _merge_mlp(x2, mod_x, f_x, conv_x, at_x, gates_x, conv_w[l], w_f_b, w_c_b, w_a_b, w_o_b,
                        g2, w1_b, w2_b, fg, seq, TM_LATENT, final_norm=last)
        if not last:
            f_c = _fourier_context(uf_c, batch, ctx_len)
            at_c = _context_attention(qkv_c, batch, ctx_len)
            c2 = _merge_mlp(c2, mod_c, f_c, conv_c, at_c, gates_c, conv_w[l], w_f_b, w_c_b, w_a_b, w_o_b,
                            g2, w1_b, w2_b, fg, ctx_len, ctx_len, final_norm=False)

    return x2.reshape(batch, seq, d)
```

```python
import functools
import math

import numpy as np
import jax
import jax.numpy as jnp
from jax import lax
from jax.experimental import pallas as pl
from jax.experimental.pallas import tpu as pltpu

D_MODEL = 1024
DEPTH = 2
GRID_W = 64
HEAD_DIM = 64
N_NA_HEADS = 8
NA_WIDTH = N_NA_HEADS * HEAD_DIM
N_FOURIER_GROUPS = 4
FOURIER_WIDTH = D_MODEL // 4
FOURIER_GROUP = FOURIER_WIDTH // N_FOURIER_GROUPS
CONV_WIDTH = D_MODEL // 4
CONV_K = 3
WIN_ROWS = 8
WIN_COLS = 16
MLP_HIDDEN = 4 * D_MODEL
N_MOD = 6
EPS = 1e-6
NEG = -1e30
LOG2E = math.log2(math.e)

SEG_FOURIER = (0, FOURIER_WIDTH)
SEG_CONV = (SEG_FOURIER[1], SEG_FOURIER[1] + 3 * CONV_WIDTH)
SEG_QKV = (SEG_CONV[1], SEG_CONV[1] + 3 * NA_WIDTH)
SEG_GATES = (SEG_QKV[1], SEG_QKV[1] + 3 * D_MODEL)
KV_START = SEG_QKV[0] + NA_WIDTH
KV_END = SEG_QKV[1]

LANES = 128
MXU_DIM = 256
VMEM_LIMIT_BYTES = 56 * 1024 * 1024

TM_LATENT = 512
ROWS_PER_ATTN_STEP = 4
ATTN_Q = ROWS_PER_ATTN_STEP * GRID_W
ATTN_KEY_BLOCKS = 3
ATTN_LOGITS_AHEAD = 1
ATTN_BLOCKS_PER_STEP = 1
DFT_N1 = 64
DFT_ROWS_PER_GROUP = 8
DFT_STAGE1_GROUPS = 4
DFT_STAGE2_GROUPS = 2
MOD_COL_TILE = 1536
PROJ_COL_CHUNK = 512
MLP_HIDDEN_CHUNK = 1024
MERGE_ROW_CHUNK = 256

BF16 = jnp.bfloat16
F32 = jnp.float32


def _params(*semantics):
    return pltpu.CompilerParams(dimension_semantics=semantics, vmem_limit_bytes=VMEM_LIMIT_BYTES)


def _dot(a, b):
    return jnp.dot(a, b, preferred_element_type=F32)


def _dot_nt(a, b):
    return lax.dot_general(a, b, (((1,), (1,)), ((), ())), preferred_element_type=F32)


def _channel_dft_tables():
    c = np.arange(FOURIER_GROUP)
    ang = 2.0 * np.pi * ((c[:, None] * c[None, :]) % FOURIER_GROUP) / FOURIER_GROUP
    eye = np.eye(N_FOURIER_GROUPS)
    s = 1.0 / math.sqrt(FOURIER_GROUP)
    return np.stack([np.kron(eye, np.cos(ang) * s), np.kron(eye, np.sin(ang) * s)]).astype(np.float32)


def _latent_dft_tables(n):
    n1, n2 = DFT_N1, n // DFT_N1
    a = np.arange(n1)
    ang1 = 2.0 * np.pi * ((a[:, None] * a[None, :]) % n1) / n1
    s1 = 1.0 / math.sqrt(n1)
    f1 = np.concatenate([np.cos(ang1) * s1, -np.sin(ang1) * s1], axis=0)
    k1 = np.arange(n1)[:, None, None]
    k2 = np.arange(n2)[None, :, None]
    m2 = np.arange(n2)[None, None, :]
    ang = 2.0 * np.pi * ((m2 * (k1 + n1 * k2)) % n) / n
    s2 = 1.0 / math.sqrt(n2)
    tr, ti = np.cos(ang) * s2, -np.sin(ang) * s2
    g = np.concatenate([np.concatenate([tr, -ti], axis=2),
                        np.concatenate([ti, tr], axis=2)], axis=1)
    return f1.astype(np.float32), g.astype(np.float32)


def _context_dft_table(n):
    a = np.arange(n)
    ang = 2.0 * np.pi * ((a[:, None] * a[None, :]) % n) / n
    s = 1.0 / math.sqrt(n)
    return np.concatenate([np.cos(ang) * s, -np.sin(ang) * s], axis=0).astype(np.float32)


def _attn_key_block_start(rb, n_row_blocks):
    return jnp.clip(rb - 1, 0, n_row_blocks - ATTN_KEY_BLOCKS)


def _attn_bias_table(rel_bias, rows):
    depth, h, ndr, _ = rel_bias.shape
    w = GRID_W
    rq, nkr = ROWS_PER_ATTN_STEP, ROWS_PER_ATTN_STEP * ATTN_KEY_BLOCKS
    ends = w - WIN_COLS
    ext = jnp.concatenate([
        jnp.broadcast_to(rel_bias[..., :1], (depth, h, ndr, ends)), rel_bias,
        jnp.broadcast_to(rel_bias[..., -1:], (depth, h, ndr, ends)), jnp.zeros((depth, h, ndr, 1), F32)], axis=-1)
    return pl.pallas_call(
        functools.partial(_bias_table_kernel, rows=rows),
        grid=(depth, h),
        in_specs=[pl.BlockSpec((None, None, ndr, 2 * w), lambda l, hh: (l, hh, 0, 0))],
        out_specs=pl.BlockSpec((None, 3, None, rq * w, nkr * w), lambda l, hh: (l, 0, hh, 0, 0)),
        out_shape=jax.ShapeDtypeStruct((depth, 3, h, rq * w, nkr * w), F32),
        compiler_params=_params("parallel", "parallel"),
        name="attn_bias_table",
    )(ext)


def _bias_table_kernel(ext_ref, o_ref, *, rows):
    w = GRID_W
    rq, nkr = ROWS_PER_ATTN_STEP, ROWS_PER_ATTN_STEP * ATTN_KEY_BLOCKS
    q = lax.broadcasted_iota(jnp.int32, (w, 2 * w), 0)
    t = lax.broadcasted_iota(jnp.int32, (w, 2 * w), 1)
    first_col = jnp.clip(q - WIN_COLS // 2, 0, w - WIN_COLS)
    neg = jnp.full((w, 2 * w), NEG, F32)
    left, right = [], []
    for d in range(2 * WIN_ROWS - 1):
        row = jnp.broadcast_to(ext_ref[d:d + 1, :] * LOG2E, (w, 2 * w))
        lo = pltpu.roll(row, w + 1, 1, stride=1, stride_axis=0)
        hi = pltpu.roll(row, 1, 1, stride=1, stride_axis=0)
        left.append(jnp.where((t >= first_col) & (t < first_col + WIN_COLS), lo, neg))
        right.append(jnp.where((t - w >= first_col) & (t - w < first_col + WIN_COLS), hi, neg))

    variants = [(0, 0), (rq, 0), (rows - rq, rows - nkr)]
    for v, (r0, ks) in enumerate(variants):
        for qi in range(rq):
            qr = r0 + qi
            rs = min(max(qr - WIN_ROWS // 2, 0), rows - WIN_ROWS)
            lead = rs - ks
            d0 = rs - qr + WIN_ROWS - 1

            def slab(kj, side):
                return side[d0 + kj - lead] if lead <= kj < lead + WIN_ROWS else neg

            for p in range(nkr // 2):
                tile = jnp.where(t < w, slab(2 * p, left), slab(2 * p + 1, right))
                o_ref[v, qi * w:(qi + 1) * w, p * 2 * w:(p + 1) * 2 * w] = tile


def _mod_kernel(c_ref, w_ref, b_ref, o_ref):
    c = c_ref[...]
    s = c * (1.0 / (1.0 + jnp.exp(-c)))
    o_ref[...] = jnp.dot(s, w_ref[...], precision=lax.Precision.HIGHEST,
                         preferred_element_type=F32) + b_ref[...]


def _modulation(cond, ada_w, ada_b):
    r, d = cond.shape
    depth, _, n = ada_w.shape
    tn = MOD_COL_TILE
    return pl.pallas_call(
        _mod_kernel,
        grid=(depth, n // tn),
        in_specs=[
            pl.BlockSpec((r, d), lambda l, j: (0, 0)),
            pl.BlockSpec((None, d, tn), lambda l, j: (l, 0, j)),
            pl.BlockSpec((None, 1, tn), lambda l, j: (l, 0, j)),
        ],
        out_specs=pl.BlockSpec((None, r, tn), lambda l, j: (l, 0, j)),
        out_shape=jax.ShapeDtypeStruct((depth, r, n), F32),
        compiler_params=_params("parallel", "parallel"),
        name="adaln_modulation",
    )(cond, ada_w, ada_b.reshape(depth, 1, n))


def _norm_modulate(x, g, shift, scale):
    ms = jnp.mean(x * x, axis=-1, keepdims=True)
    return (x * lax.rsqrt(ms + EPS)) * (g * (1.0 + scale)) + shift


def _twice_sigmoid(g_bf16):
    return jnp.tanh((g_bf16 * 0.5).astype(F32)) + 1.0


def _inproj_kernel(x_ref, mod_ref, g_ref, w_ref, *out_refs, segments, shift_row, scale_row):
    h = _norm_modulate(x_ref[...], g_ref[...],
                       mod_ref[0, shift_row:shift_row + 1, :], mod_ref[0, scale_row:scale_row + 1, :])
    hb = h.astype(BF16)
    for o_ref, (lo, hi) in zip(out_refs, segments):
        for c0 in range(lo, hi, PROJ_COL_CHUNK):
            c1 = min(c0 + PROJ_COL_CHUNK, hi)
            o_ref[:, c0 - lo:c1 - lo] = _dot(hb, w_ref[:, c0:c1]).astype(o_ref.dtype)


def _in_projection(x2d, mod, g, w_bf16, segments, seg_dtypes, seq, tm, shift_row, scale_row):
    m, d = x2d.shape
    n = w_bf16.shape[1]
    tiles_per_seq = seq // tm
    kern = functools.partial(_inproj_kernel, segments=tuple(segments),
                             shift_row=shift_row, scale_row=scale_row)
    return pl.pallas_call(
        kern,
        grid=(m // tm,),
        in_specs=[
            pl.BlockSpec((tm, d), lambda i: (i, 0)),
            pl.BlockSpec((1, N_MOD, d), lambda i: (i // tiles_per_seq, 0, 0)),
            pl.BlockSpec((1, d), lambda i: (0, 0)),
            pl.BlockSpec((d, n), lambda i: (0, 0), pipeline_mode=pl.Buffered(1)),
        ],
        out_specs=[pl.BlockSpec((tm, hi - lo), lambda i: (i, 0)) for lo, hi in segments],
        out_shape=[jax.ShapeDtypeStruct((m, hi - lo), dt) for (lo, hi), dt in zip(segments, seg_dtypes)],
        compiler_params=_params("parallel"),
        name="norm_inproj",
    )(x2d, mod, g, w_bf16)


def _dft_stage1_kernel(x_ref, f_ref, o_ref):
    n1, rows, c = x_ref.shape
    t = DFT_ROWS_PER_GROUP
    f = f_ref[...].astype(BF16)
    for g in range(rows // t):
        x = x_ref[:, g * t:(g + 1) * t, :].reshape(n1 * t, c).astype(BF16)
        o_ref[:, :, g * t:(g + 1) * t, :] = _dot(f, x).reshape(2, n1, t, c)


def _dft_stage2_kernel(z_ref, g_ref, cs_ref, o_ref, *, n2):
    cos_t, sin_t = cs_ref[0].astype(BF16), cs_ref[1].astype(BF16)
    c = z_ref.shape[3]
    for j in range(z_ref.shape[1]):
        zz = z_ref[:, j].reshape(2 * n2, c).astype(BF16)
        p = _dot(g_ref[j].astype(BF16), zz).astype(BF16)
        o_ref[:, j, :] = _dot(p[:n2], cos_t) + _dot(p[n2:], sin_t)


def _fourier_latent(u, batch, seq):
    c = u.shape[1]
    n1, n2 = DFT_N1, seq // DFT_N1
    f1_np, g_np = _latent_dft_tables(seq)
    t = DFT_ROWS_PER_GROUP
    r1, r2 = t * DFT_STAGE1_GROUPS, t * DFT_STAGE2_GROUPS
    f1 = jnp.asarray(np.kron(f1_np, np.eye(t, dtype=np.float32)))
    g = jnp.asarray(g_np)
    cs = jnp.asarray(_channel_dft_tables())
    z = pl.pallas_call(
        _dft_stage1_kernel,
        grid=(batch, n2 // r1),
        in_specs=[
            pl.BlockSpec((None, n1, r1, c), lambda b, j: (b, 0, j, 0)),
            pl.BlockSpec((2 * n1 * t, n1 * t), lambda b, j: (0, 0)),
        ],
        out_specs=pl.BlockSpec((None, 2, n1, r1, c), lambda b, j: (b, 0, 0, j, 0)),
        out_shape=jax.ShapeDtypeStruct((batch, 2, n1, n2, c), F32),
        compiler_params=_params("parallel", "parallel"),
        name="dft_stage1",
    )(u.reshape(batch, n1, n2, c), f1)
    y = pl.pallas_call(
        functools.partial(_dft_stage2_kernel, n2=n2),
        grid=(n1 // r2, batch),
        in_specs=[
            pl.BlockSpec((None, 2, r2, n2, c), lambda k, b: (b, 0, k, 0, 0)),
            pl.BlockSpec((r2, 2 * n2, 2 * n2), lambda k, b: (k, 0, 0)),
            pl.BlockSpec((2, c, c), lambda k, b: (0, 0, 0)),
        ],
        out_specs=pl.BlockSpec((None, n2, r2, c), lambda k, b: (b, 0, k, 0)),
        out_shape=jax.ShapeDtypeStruct((batch, n2, n1, c), F32),
        compiler_params=_params("parallel", "parallel"),
        name="dft_stage2",
    )(z, g, cs)
    return y.reshape(batch * seq, c)


def _dft_context_kernel(u_ref, f_ref, cs_ref, o_ref, *, n):
    p = _dot(f_ref[...].astype(BF16), u_ref[...].astype(BF16)).astype(BF16)
    y = _dot(p[:n], cs_ref[0].astype(BF16)) + _dot(p[n:], cs_ref[1].astype(BF16))
    o_ref[...] = y.astype(o_ref.dtype)


def _fourier_context(u, batch, seq):
    c = u.shape[1]
    f = jnp.asarray(_context_dft_table(seq))
    cs = jnp.asarray(_channel_dft_tables())
    return pl.pallas_call(
        functools.partial(_dft_context_kernel, n=seq),
        grid=(batch,),
        in_specs=[
            pl.BlockSpec((seq, c), lambda b: (b, 0)),
            pl.BlockSpec((2 * seq, seq), lambda b: (0, 0)),
            pl.BlockSpec((2, c, c), lambda b: (0, 0, 0)),
        ],
        out_specs=pl.BlockSpec((seq, c), lambda b: (b, 0)),
        out_shape=jax.ShapeDtypeStruct((batch * seq, c), BF16),
        compiler_params=_params("parallel"),
        name="dft_context",
    )(u, f, cs)


def _attend_heads(problems):
    q_scale = HEAD_DIM ** -0.5 * LOG2E

    def lanes(h):
        return slice((h // 2) * LANES, (h // 2 + 1) * LANES)

    def own_lanes(shape, h):
        low = lax.broadcasted_iota(jnp.int32, shape, 1) < HEAD_DIM
        return low if h % 2 == 0 else jnp.logical_not(low)

    def logits(unit):
        h, g = unit
        q_ref, key_refs, _, bias_fn, _ = problems[g]
        q_pair = q_ref[:, lanes(h)]
        qs = (q_pair.astype(F32) * q_scale).astype(BF16)
        qm = jnp.where(own_lanes(qs.shape, h), qs, jnp.zeros_like(qs))
        out = []
        for j, k_ref in enumerate(key_refs):
            sj = _dot_nt(qm, k_ref[:, lanes(h)])
            bj = bias_fn(h, j)
            out.append(sj if bj is None else sj + bj)
        return out

    units = [(h, g) for h in range(N_NA_HEADS) for g in range(len(problems))]
    ahead = ATTN_LOGITS_AHEAD
    queue = [logits(u) for u in units[:ahead]]
    even_out = [None] * len(problems)
    for n, (h, g) in enumerate(units):
        s = queue.pop(0)
        if n + ahead < len(units):
            queue.append(logits(units[n + ahead]))
        _, _, value_refs, _, o_ref = problems[g]
        m = s[0]
        for sj in s[1:]:
            m = jnp.maximum(m, sj)
        m = jnp.max(m, axis=-1, keepdims=True)
        p = jnp.concatenate([jnp.exp2(sj - m).astype(BF16) for sj in s], axis=1)
        v = jnp.concatenate([v_ref[:, lanes(h)] for v_ref in value_refs], axis=0)
        v_aug = jnp.where(own_lanes(v.shape, h), v, jnp.ones_like(v))
        acc = _dot(p, v_aug)
        denom = pltpu.roll(acc, HEAD_DIM, 1)
        out = acc * (1.0 / denom)
        if h % 2 == 0:
            even_out[g] = out
        else:
            o_ref[:, lanes(h)] = jnp.where(own_lanes(out.shape, 0), even_out[g], out).astype(o_ref.dtype)


def _natten_kernel(q_ref, *refs):
    nb, ng, tq = ATTN_KEY_BLOCKS, ATTN_BLOCKS_PER_STEP, ATTN_Q
    kv_refs, (kc_ref, vc_ref), bias_refs, o_ref = (
        refs[:2 * nb * ng], refs[2 * nb * ng:2 * nb * ng + 2], refs[2 * nb * ng + 2:-1], refs[-1])

    def problem(g):
        k_refs = kv_refs[2 * nb * g:2 * nb * g + nb]
        v_refs = kv_refs[2 * nb * g + nb:2 * nb * (g + 1)]
        bias_ref = bias_refs[g]

        def bias_fn(h, j):
            return bias_ref[h, :, j * tq:(j + 1) * tq] if j < nb else None

        rows = pl.ds(g * tq, tq)
        return (q_ref.at[rows], list(k_refs) + [kc_ref], list(v_refs) + [vc_ref], bias_fn, o_ref.at[rows])

    _attend_heads([problem(g) for g in range(ng)])


def _neighbourhood_attention(qkv, kv_ctx, ctx_k_col, bias_tables, layer, batch, seq, ctx_len):
    m = qkv.shape[0]
    tq, ng = ATTN_Q, ATTN_BLOCKS_PER_STEP
    nrb = seq // tq
    assert ctx_len == tq, "context keys are processed as one key block"

    def q_map(b, r):
        return (b * (nrb // ng) + r, 0)

    def kv_map(g, j, col):
        return lambda b, r: (b * nrb + _attn_key_block_start(ng * r + g, nrb) + j, col)

    def bias_map(g):
        def index(b, r):
            rb = ng * r + g
            return (layer, jnp.where(rb == 0, 0, jnp.where(rb == nrb - 1, 2, 1)), 0, 0, 0)
        return index

    blk = (tq, NA_WIDTH)
    in_specs = [pl.BlockSpec((ng * tq, NA_WIDTH), q_map)]
    for g in range(ng):
        in_specs += [pl.BlockSpec(blk, kv_map(g, j, 1)) for j in range(ATTN_KEY_BLOCKS)]
        in_specs += [pl.BlockSpec(blk, kv_map(g, j, 2)) for j in range(ATTN_KEY_BLOCKS)]
    in_specs += [
        pl.BlockSpec((ctx_len, NA_WIDTH), lambda b, r: (b, ctx_k_col)),
        pl.BlockSpec((ctx_len, NA_WIDTH), lambda b, r: (b, ctx_k_col + 1)),
    ]
    in_specs += [pl.BlockSpec((None, None, N_NA_HEADS, tq, ATTN_KEY_BLOCKS * tq), bias_map(g))
                 for g in range(ng)]
    args = [qkv] * (1 + 2 * ATTN_KEY_BLOCKS * ng) + [kv_ctx, kv_ctx] + [bias_tables] * ng
    return pl.pallas_call(
        _natten_kernel,
        grid=(batch, nrb // ng),
        in_specs=in_specs,
        out_specs=pl.BlockSpec((ng * tq, NA_WIDTH), q_map),
        out_shape=jax.ShapeDtypeStruct((m, NA_WIDTH), BF16),
        compiler_params=_params("parallel", "arbitrary"),
        name="neighbourhood_attention",
    )(*args)


def _ctx_attn_kernel(q_ref, k_ref, v_ref, o_ref):
    _attend_heads([(q_ref, [k_ref], [v_ref], lambda h, j: None, o_ref)])


def _context_attention(qkv, batch, ctx_len):
    blk = (ctx_len, NA_WIDTH)
    return pl.pallas_call(
        _ctx_attn_kernel,
        grid=(batch,),
        in_specs=[pl.BlockSpec(blk, lambda b: (b, 0)),
                  pl.BlockSpec(blk, lambda b: (b, 1)),
                  pl.BlockSpec(blk, lambda b: (b, 2))],
        out_specs=pl.BlockSpec(blk, lambda b: (b, 0)),
        out_shape=jax.ShapeDtypeStruct((batch * ctx_len, NA_WIDTH), BF16),
        compiler_params=_params("parallel"),
        name="context_attention",
    )(qkv, qkv, qkv)


HALO_ROWS = 16


def _merge_mlp_kernel(x_ref, mod_ref, f_ref, cv_ref, cvp_ref, cvn_ref, at_ref, gt_ref, cw_ref,
                      wf_ref, wc_ref, wa_ref, wo_ref, g2_ref, w1_ref, w2_ref, fg_ref, o_ref,
                      cv_scr, x1_scr, *, tiles_per_seq, final_norm):
    i = pl.program_id(0)
    tm = x_ref.shape[0]
    cw = CONV_WIDTH
    has_prev = (i % tiles_per_seq) != 0
    has_next = (i % tiles_per_seq) != tiles_per_seq - 1

    u = cv_ref[:, 0:cw].astype(F32)
    gb = cv_ref[:, cw:2 * cw].astype(F32)
    gc = cv_ref[:, 2 * cw:3 * cw].astype(F32)
    z = gc * u
    zp_row = (cvp_ref[HALO_ROWS - 1:HALO_ROWS, 2 * cw:3 * cw].astype(F32)
              * cvp_ref[HALO_ROWS - 1:HALO_ROWS, 0:cw].astype(F32))
    zn_row = cvn_ref[0:1, 2 * cw:3 * cw].astype(F32) * cvn_ref[0:1, 0:cw].astype(F32)
    zp_row = jnp.where(has_prev, zp_row, 0.0)
    zn_row = jnp.where(has_next, zn_row, 0.0)
    row = lax.broadcasted_iota(jnp.int32, (tm, cw), 0)
    z_prev = jnp.where(row == 0, zp_row, pltpu.roll(z, 1, 0))
    z_next = jnp.where(row == tm - 1, zn_row, pltpu.roll(z, tm - 1, 0))
    y = cw_ref[0:1, :] * z_prev + cw_ref[1:2, :] * z + cw_ref[2:3, :] * z_next
    cv_scr[...] = (gb * y).astype(BF16)

    d = D_MODEL
    gate = mod_ref[0, 2:3, :]

    def branches(c):
        r = slice(c * MERGE_ROW_CHUNK, (c + 1) * MERGE_ROW_CHUNK)
        return (_dot(f_ref[r, :].astype(BF16), wf_ref[...]), _dot(cv_scr[r, :], wc_ref[...]),
                _dot(at_ref[r, :], wa_ref[...]))

    nxt = branches(0)
    n_chunks = tm // MERGE_ROW_CHUNK
    for c in range(n_chunks):
        r = slice(c * MERGE_ROW_CHUNK, (c + 1) * MERGE_ROW_CHUNK)
        pf, pc, pa = nxt
        if c + 1 < n_chunks:
            nxt = branches(c + 1)
        mix = _twice_sigmoid(gt_ref[r, 0:d]) * pf
        mix = mix + _twice_sigmoid(gt_ref[r, d:2 * d]) * pc
        mix = mix + _twice_sigmoid(gt_ref[r, 2 * d:3 * d]) * pa
        proj = _dot(mix.astype(BF16), wo_ref[...])
        x1_scr[r, :] = x_ref[r, :] + (0.5 * gate) * proj

    x1 = x1_scr[...]
    h = _norm_modulate(x1, g2_ref[...], mod_ref[0, 3:4, :], mod_ref[0, 4:5, :]).astype(BF16)
    acc = None
    for c0 in range(0, MLP_HIDDEN, MLP_HIDDEN_CHUNK):
        a = jnp.maximum(_dot(h, w1_ref[:, c0:c0 + MLP_HIDDEN_CHUNK]), 0.0)
        part = _dot((a * a).astype(BF16), w2_ref[c0:c0 + MLP_HIDDEN_CHUNK, :])
        acc = part if acc is None else acc + part
    y = x1 + mod_ref[0, 5:6, :] * acc
    if final_norm:
        ms = jnp.mean(y * y, axis=-1, keepdims=True)
        y = y * lax.rsqrt(ms + EPS) * fg_ref[...]
    o_ref[...] = y


def _merge_mlp(x2d, mod, f, conv, attn, gates, conv_w, w_f, w_c, w_a, w_o, g2, w1, w2, final_g,
               seq, tm, final_norm):
    m, d = x2d.shape
    tiles_per_seq = seq // tm
    halo_per_tile = tm // HALO_ROWS
    n_halo = m // HALO_ROWS
    kern = functools.partial(_merge_mlp_kernel, tiles_per_seq=tiles_per_seq, final_norm=final_norm)
    const = lambda i: (0, 0)
    resident = dict(pipeline_mode=pl.Buffered(1))
    return pl.pallas_call(
        kern,
        grid=(m // tm,),
        in_specs=[
            pl.BlockSpec((tm, d), lambda i: (i, 0)),
            pl.BlockSpec((1, N_MOD, d), lambda i: (i // tiles_per_seq, 0, 0)),
            pl.BlockSpec((tm, FOURIER_WIDTH), lambda i: (i, 0)),
            pl.BlockSpec((tm, 3 * CONV_WIDTH), lambda i: (i, 0)),
            pl.BlockSpec((HALO_ROWS, 3 * CONV_WIDTH),
                         lambda i: (jnp.maximum(i * halo_per_tile - 1, 0), 0)),
            pl.BlockSpec((HALO_ROWS, 3 * CONV_WIDTH),
                         lambda i: (jnp.minimum((i + 1) * halo_per_tile, n_halo - 1), 0)),
            pl.BlockSpec((tm, NA_WIDTH), lambda i: (i, 0)),
            pl.BlockSpec((tm, 3 * d), lambda i: (i, 0)),
            pl.BlockSpec((CONV_K, CONV_WIDTH), const),
            pl.BlockSpec((FOURIER_WIDTH, d), const, **resident),
            pl.BlockSpec((CONV_WIDTH, d), const, **resident),
            pl.BlockSpec((NA_WIDTH, d), const, **resident),
            pl.BlockSpec((d, d), const, **resident),
            pl.BlockSpec((1, d), const),
            pl.BlockSpec((d, MLP_HIDDEN), const, **resident),
            pl.BlockSpec((MLP_HIDDEN, d), const, **resident),
            pl.BlockSpec((1, d), const),
        ],
        out_specs=pl.BlockSpec((tm, d), lambda i: (i, 0)),
        out_shape=jax.ShapeDtypeStruct((m, d), F32),
        scratch_shapes=[pltpu.VMEM((tm, CONV_WIDTH), BF16), pltpu.VMEM((tm, d), F32)],
        compiler_params=_params("parallel"),
        name="merge_mlp",
    )(x2d, mod, f, conv, conv, conv, attn, gates, conv_w, w_f, w_c, w_a, w_o, g2, w1, w2, final_g)


ALL_SEGMENTS = (SEG_FOURIER, SEG_CONV, SEG_QKV, SEG_GATES)
SEGMENT_DTYPES = (F32, BF16, BF16, BF16)


def kernel(x, c, ctx, c_ctx, ada_w, ada_b, norm1_g, norm2_g, w_in, conv_w, rel_bias,
           w_fourier, w_conv, w_attn, w_o, mlp_w1, mlp_w2, final_g):
    batch, seq, d = x.shape
    ctx_len = ctx.shape[1]
    rows = seq // GRID_W
    depth = ada_w.shape[0]

    n_cond = batch + 1
    pad = (-n_cond) % 8
    cond = jnp.concatenate([c, c_ctx[None, :], jnp.zeros((pad, d), F32)], axis=0)
    mod_all = _modulation(cond, ada_w, ada_b)

    x2 = x.reshape(batch * seq, d)
    c2 = ctx.reshape(batch * ctx_len, d)
    fg = final_g.reshape(1, d)
    bias_tables = _attn_bias_table(rel_bias, rows)

    for l in range(depth):
        last = l == depth - 1
        mod_x = mod_all[l, :batch].reshape(batch, N_MOD, d)
        mod_c = jnp.broadcast_to(mod_all[l, batch].reshape(1, N_MOD, d), (batch, N_MOD, d))
        g1 = norm1_g[l].reshape(1, d)
        g2 = norm2_g[l].reshape(1, d)
        w_in_b = w_in[l].astype(BF16)
        w_f_b = w_fourier[l].astype(BF16)
        w_c_b = w_conv[l].astype(BF16)
        w_a_b = w_attn[l].astype(BF16)
        w_o_b = w_o[l].astype(BF16)
        w1_b = mlp_w1[l].astype(BF16)
        w2_b = mlp_w2[l].astype(BF16)

        if last:
            (kv_c,) = _in_projection(c2, mod_c, g1, w_in_b[:, KV_START:KV_END],
                                     ((0, KV_END - KV_START),), (BF16,), ctx_len, ctx_len, 0, 1)
            ctx_k_col = 0
        else:
            uf_c, conv_c, qkv_c, gates_c = _in_projection(
                c2, mod_c, g1, w_in_b, ALL_SEGMENTS, SEGMENT_DTYPES, ctx_len, ctx_len, 0, 1)
            kv_c, ctx_k_col = qkv_c, 1

        uf_x, conv_x, qkv_x, gates_x = _in_projection(
            x2, mod_x, g1, w_in_b, ALL_SEGMENTS, SEGMENT_DTYPES, seq, TM_LATENT, 0, 1)
        f_x = _fourier_latent(uf_x, batch, seq)
        at_x = _neighbourhood_attention(qkv_x, kv_c, ctx_k_col, bias_tables, l, batch, seq, ctx_len)
        x2 = _merge_mlp(x2, mod_x, f_x, conv_x, at_x, gates_x, conv_w[l], w_f_b, w_c_b, w_a_b, w_o_b,
                        g2, w1_b, w2_b, fg, seq, TM_LATENT, final_norm=last)
        if not last:
            f_c = _fourier_context(uf_c, batch, ctx_len)
            at_c = _context_attention(qkv_c, batch, ctx_len)
            c2 = _merge_mlp(c2, mod_c, f_c, conv_c, at_c, gates_c, conv_w[l], w_f_b, w_c_b, w_a_b, w_o_b,
                            g2, w1_b, w2_b, fg, ctx_len, ctx_len, final_norm=False)

    return x2.reshape(batch, seq, d)
```

```python
import functools
import math

import numpy as np
import jax
import jax.numpy as jnp
from jax import lax
from jax.experimental import pallas as pl
from jax.experimental.pallas import tpu as pltpu

D_MODEL = 1024
DEPTH = 2
GRID_W = 64
HEAD_DIM = 64
N_NA_HEADS = 8
NA_WIDTH = N_NA_HEADS * HEAD_DIM
N_FOURIER_GROUPS = 4
FOURIER_WIDTH = D_MODEL // 4
FOURIER_GROUP = FOURIER_WIDTH // N_FOURIER_GROUPS
CONV_WIDTH = D_MODEL // 4
CONV_K = 3
WIN_ROWS = 8
WIN_COLS = 16
MLP_HIDDEN = 4 * D_MODEL
N_MOD = 6
EPS = 1e-6
NEG = -1e30
LOG2E = math.log2(math.e)

SEG_FOURIER = (0, FOURIER_WIDTH)
SEG_CONV = (SEG_FOURIER[1], SEG_FOURIER[1] + 3 * CONV_WIDTH)
SEG_QKV = (SEG_CONV[1], SEG_CONV[1] + 3 * NA_WIDTH)
SEG_GATES = (SEG_QKV[1], SEG_QKV[1] + 3 * D_MODEL)
KV_START = SEG_QKV[0] + NA_WIDTH
KV_END = SEG_QKV[1]

LANES = 128
VMEM_LIMIT_BYTES = 56 * 1024 * 1024

TM_LATENT = 512
TM_INPROJ = 1024
ROWS_PER_ATTN_STEP = 4
ATTN_Q = ROWS_PER_ATTN_STEP * GRID_W
ATTN_KEY_BLOCKS = 3
ATTN_LOGITS_AHEAD = 1
ATTN_BLOCKS_PER_STEP = 1
DFT_N1 = 64
DFT_ROWS_PER_GROUP = 8
DFT_STAGE1_GROUPS = 4
DFT_STAGE2_GROUPS = 2
MOD_COL_TILE = 1536
PROJ_COL_CHUNK = 512
MLP_HIDDEN_CHUNK = 1024
MERGE_ROW_CHUNK = 256

BF16 = jnp.bfloat16
F32 = jnp.float32


def _params(*semantics):
    return pltpu.CompilerParams(dimension_semantics=semantics, vmem_limit_bytes=VMEM_LIMIT_BYTES)


def _dot(a, b):
    return jnp.dot(a, b, preferred_element_type=F32)


def _dot_nt(a, b):
    return lax.dot_general(a, b, (((1,), (1,)), ((), ())), preferred_element_type=F32)


def _channel_dft_tables():
    c = np.arange(FOURIER_GROUP)
    ang = 2.0 * np.pi * ((c[:, None] * c[None, :]) % FOURIER_GROUP) / FOURIER_GROUP
    eye = np.eye(N_FOURIER_GROUPS)
    s = 1.0 / math.sqrt(FOURIER_GROUP)
    return np.stack([np.kron(eye, np.cos(ang) * s), np.kron(eye, np.sin(ang) * s)]).astype(np.float32)


def _latent_dft_tables(n):
    n1, n2 = DFT_N1, n // DFT_N1
    a = np.arange(n1)
    ang1 = 2.0 * np.pi * ((a[:, None] * a[None, :]) % n1) / n1
    s1 = 1.0 / math.sqrt(n1)
    f1 = np.concatenate([np.cos(ang1) * s1, -np.sin(ang1) * s1], axis=0)
    k1 = np.arange(n1)[:, None, None]
    k2 = np.arange(n2)[None, :, None]
    m2 = np.arange(n2)[None, None, :]
    ang = 2.0 * np.pi * ((m2 * (k1 + n1 * k2)) % n) / n
    s2 = 1.0 / math.sqrt(n2)
    tr, ti = np.cos(ang) * s2, -np.sin(ang) * s2
    g = np.concatenate([np.concatenate([tr, -ti], axis=2),
                        np.concatenate([ti, tr], axis=2)], axis=1)
    return f1.astype(np.float32), g.astype(np.float32)


def _context_dft_table(n):
    a = np.arange(n)
    ang = 2.0 * np.pi * ((a[:, None] * a[None, :]) % n) / n
    s = 1.0 / math.sqrt(n)
    return np.concatenate([np.cos(ang) * s, -np.sin(ang) * s], axis=0).astype(np.float32)


def _attn_key_block_start(rb, n_row_blocks):
    lead_blocks = (WIN_ROWS // 2) // ROWS_PER_ATTN_STEP
    return jnp.clip(rb - lead_blocks, 0, n_row_blocks - ATTN_KEY_BLOCKS)


def _attn_bias_table(rel_bias, rows):
    depth, h, ndr, _ = rel_bias.shape
    w = GRID_W
    rq, nkr = ROWS_PER_ATTN_STEP, ROWS_PER_ATTN_STEP * ATTN_KEY_BLOCKS
    ends = w - WIN_COLS
    ext = jnp.concatenate([
        jnp.broadcast_to(rel_bias[..., :1], (depth, h, ndr, ends)), rel_bias,
        jnp.broadcast_to(rel_bias[..., -1:], (depth, h, ndr, ends)), jnp.zeros((depth, h, ndr, 1), F32)], axis=-1)
    return pl.pallas_call(
        functools.partial(_bias_table_kernel, rows=rows),
        grid=(depth, h),
        in_specs=[pl.BlockSpec((None, None, ndr, 2 * w), lambda l, hh: (l, hh, 0, 0))],
        out_specs=pl.BlockSpec((None, 3, None, rq * w, nkr * w), lambda l, hh: (l, 0, hh, 0, 0)),
        out_shape=jax.ShapeDtypeStruct((depth, 3, h, rq * w, nkr * w), F32),
        compiler_params=_params("parallel", "parallel"),
        name="attn_bias_table",
    )(ext)


def _bias_table_kernel(ext_ref, o_ref, *, rows):
    w = GRID_W
    rq, nkr = ROWS_PER_ATTN_STEP, ROWS_PER_ATTN_STEP * ATTN_KEY_BLOCKS
    q = lax.broadcasted_iota(jnp.int32, (w, 2 * w), 0)
    t = lax.broadcasted_iota(jnp.int32, (w, 2 * w), 1)
    first_col = jnp.clip(q - WIN_COLS // 2, 0, w - WIN_COLS)
    neg = jnp.full((w, 2 * w), NEG, F32)
    left, right = [], []
    for d in range(2 * WIN_ROWS - 1):
        row = jnp.broadcast_to(ext_ref[d:d + 1, :] * LOG2E, (w, 2 * w))
        lo = pltpu.roll(row, w + 1, 1, stride=1, stride_axis=0)
        hi = pltpu.roll(row, 1, 1, stride=1, stride_axis=0)
        left.append(jnp.where((t >= first_col) & (t < first_col + WIN_COLS), lo, neg))
        right.append(jnp.where((t - w >= first_col) & (t - w < first_col + WIN_COLS), hi, neg))

    variants = [(0, 0), (rq, 0), (rows - rq, rows - nkr)]
    for v, (r0, ks) in enumerate(variants):
        for qi in range(rq):
            qr = r0 + qi
            rs = min(max(qr - WIN_ROWS // 2, 0), rows - WIN_ROWS)
            lead = rs - ks
            d0 = rs - qr + WIN_ROWS - 1

            def slab(kj, side):
                return side[d0 + kj - lead] if lead <= kj < lead + WIN_ROWS else neg

            for p in range(nkr // 2):
                tile = jnp.where(t < w, slab(2 * p, left), slab(2 * p + 1, right))
                o_ref[v, qi * w:(qi + 1) * w, p * 2 * w:(p + 1) * 2 * w] = tile


def _mod_kernel(c_ref, w_ref, b_ref, o_ref):
    c = c_ref[...]
    s = c * (1.0 / (1.0 + jnp.exp(-c)))
    o_ref[...] = jnp.dot(s, w_ref[...], precision=lax.Precision.HIGHEST,
                         preferred_element_type=F32) + b_ref[...]


def _modulation(cond, ada_w, ada_b):
    r, d = cond.shape
    depth, _, n = ada_w.shape
    tn = MOD_COL_TILE
    return pl.pallas_call(
        _mod_kernel,
        grid=(depth, n // tn),
        in_specs=[
            pl.BlockSpec((r, d), lambda l, j: (0, 0)),
            pl.BlockSpec((None, d, tn), lambda l, j: (l, 0, j)),
            pl.BlockSpec((None, 1, tn), lambda l, j: (l, 0, j)),
        ],
        out_specs=pl.BlockSpec((None, r, tn), lambda l, j: (l, 0, j)),
        out_shape=jax.ShapeDtypeStruct((depth, r, n), F32),
        compiler_params=_params("parallel", "parallel"),
        name="adaln_modulation",
    )(cond, ada_w, ada_b.reshape(depth, 1, n))


def _norm_modulate(x, g, shift, scale):
    ms = jnp.mean(x * x, axis=-1, keepdims=True)
    return (x * lax.rsqrt(ms + EPS)) * (g * (1.0 + scale)) + shift


def _twice_sigmoid(g_bf16):
    return jnp.tanh((g_bf16 * 0.5).astype(F32)) + 1.0


def _inproj_kernel(x_ref, mod_ref, g_ref, w_ref, *out_refs, segments, shift_row, scale_row):
    h = _norm_modulate(x_ref[...], g_ref[...],
                       mod_ref[0, shift_row:shift_row + 1, :], mod_ref[0, scale_row:scale_row + 1, :])
    hb = h.astype(BF16)
    for o_ref, (lo, hi) in zip(out_refs, segments):
        for c0 in range(lo, hi, PROJ_COL_CHUNK):
            c1 = min(c0 + PROJ_COL_CHUNK, hi)
            o_ref[:, c0 - lo:c1 - lo] = _dot(hb, w_ref[:, c0:c1]).astype(o_ref.dtype)


def _in_projection(x2d, mod, g, w_bf16, segments, seg_dtypes, seq, tm, shift_row, scale_row):
    m, d = x2d.shape
    n = w_bf16.shape[1]
    tiles_per_seq = seq // tm
    kern = functools.partial(_inproj_kernel, segments=tuple(segments),
                             shift_row=shift_row, scale_row=scale_row)
    return pl.pallas_call(
        kern,
        grid=(m // tm,),
        in_specs=[
            pl.BlockSpec((tm, d), lambda i: (i, 0)),
            pl.BlockSpec((1, N_MOD, d), lambda i: (i // tiles_per_seq, 0, 0)),
            pl.BlockSpec((1, d), lambda i: (0, 0)),
            pl.BlockSpec((d, n), lambda i: (0, 0), pipeline_mode=pl.Buffered(1)),
        ],
        out_specs=[pl.BlockSpec((tm, hi - lo), lambda i: (i, 0)) for lo, hi in segments],
        out_shape=[jax.ShapeDtypeStruct((m, hi - lo), dt) for (lo, hi), dt in zip(segments, seg_dtypes)],
        compiler_params=_params("parallel"),
        name="norm_inproj",
    )(x2d, mod, g, w_bf16)


def _dft_stage1_kernel(x_ref, f_ref, o_ref):
    n1, rows, c = x_ref.shape
    t = DFT_ROWS_PER_GROUP
    f = f_ref[...].astype(BF16)
    for g in range(rows // t):
        x = x_ref[:, g * t:(g + 1) * t, :].reshape(n1 * t, c).astype(BF16)
        o_ref[:, :, g * t:(g + 1) * t, :] = _dot(f, x).reshape(2, n1, t, c)


def _dft_stage2_kernel(z_ref, g_ref, cs_ref, o_ref, *, n2):
    cos_t, sin_t = cs_ref[0].astype(BF16), cs_ref[1].astype(BF16)
    c = z_ref.shape[3]
    for j in range(z_ref.shape[1]):
        zz = z_ref[:, j].reshape(2 * n2, c).astype(BF16)
        p = _dot(g_ref[j].astype(BF16), zz).astype(BF16)
        o_ref[:, j, :] = _dot(p[:n2], cos_t) + _dot(p[n2:], sin_t)


def _fourier_latent(u, batch, seq):
    c = u.shape[1]
    n1, n2 = DFT_N1, seq // DFT_N1
    f1_np, g_np = _latent_dft_tables(seq)
    t = DFT_ROWS_PER_GROUP
    r1, r2 = t * DFT_STAGE1_GROUPS, t * DFT_STAGE2_GROUPS
    f1 = jnp.asarray(np.kron(f1_np, np.eye(t, dtype=np.float32)))
    g = jnp.asarray(g_np)
    cs = jnp.asarray(_channel_dft_tables())
    z = pl.pallas_call(
        _dft_stage1_kernel,
        grid=(batch, n2 // r1),
        in_specs=[
            pl.BlockSpec((None, n1, r1, c), lambda b, j: (b, 0, j, 0)),
            pl.BlockSpec((2 * n1 * t, n1 * t), lambda b, j: (0, 0)),
        ],
        out_specs=pl.BlockSpec((None, 2, n1, r1, c), lambda b, j: (b, 0, 0, j, 0)),
        out_shape=jax.ShapeDtypeStruct((batch, 2, n1, n2, c), F32),
        compiler_params=_params("parallel", "parallel"),
        name="dft_stage1",
    )(u.reshape(batch, n1, n2, c), f1)
    y = pl.pallas_call(
        functools.partial(_dft_stage2_kernel, n2=n2),
        grid=(n1 // r2, batch),
        in_specs=[
            pl.BlockSpec((None, 2, r2, n2, c), lambda k, b: (b, 0, k, 0, 0)),
            pl.BlockSpec((r2, 2 * n2, 2 * n2), lambda k, b: (k, 0, 0)),
            pl.BlockSpec((2, c, c), lambda k, b: (0, 0, 0)),
        ],
        out_specs=pl.BlockSpec((None, n2, r2, c), lambda k, b: (b, 0, k, 0)),
        out_shape=jax.ShapeDtypeStruct((batch, n2, n1, c), F32),
        compiler_params=_params("parallel", "parallel"),
        name="dft_stage2",
    )(z, g, cs)
    return y.reshape(batch * seq, c)


def _dft_context_kernel(u_ref, f_ref, cs_ref, o_ref, *, n):
    p = _dot(f_ref[...].astype(BF16), u_ref[...].astype(BF16)).astype(BF16)
    y = _dot(p[:n], cs_ref[0].astype(BF16)) + _dot(p[n:], cs_ref[1].astype(BF16))
    o_ref[...] = y.astype(o_ref.dtype)


def _fourier_context(u, batch, seq):
    c = u.shape[1]
    f = jnp.asarray(_context_dft_table(seq))
    cs = jnp.asarray(_channel_dft_tables())
    return pl.pallas_call(
        functools.partial(_dft_context_kernel, n=seq),
        grid=(batch,),
        in_specs=[
            pl.BlockSpec((seq, c), lambda b: (b, 0)),
            pl.BlockSpec((2 * seq, seq), lambda b: (0, 0)),
            pl.BlockSpec((2, c, c), lambda b: (0, 0, 0)),
        ],
        out_specs=pl.BlockSpec((seq, c), lambda b: (b, 0)),
        out_shape=jax.ShapeDtypeStruct((batch * seq, c), BF16),
        compiler_params=_params("parallel"),
        name="dft_context",
    )(u, f, cs)


def _attend_heads(problems):
    q_scale = HEAD_DIM ** -0.5 * LOG2E

    def lanes(h):
        return slice((h // 2) * LANES, (h // 2 + 1) * LANES)

    def own_lanes(shape, h):
        low = lax.broadcasted_iota(jnp.int32, shape, 1) < HEAD_DIM
        return low if h % 2 == 0 else jnp.logical_not(low)

    def logits(unit):
        h, g = unit
        q_ref, key_refs, _, bias_fn, _ = problems[g]
        q_pair = q_ref[:, lanes(h)]
        qs = (q_pair.astype(F32) * q_scale).astype(BF16)
        qm = jnp.where(own_lanes(qs.shape, h), qs, jnp.zeros_like(qs))
        out = []
        for j, k_ref in enumerate(key_refs):
            sj = _dot_nt(qm, k_ref[:, lanes(h)])
            bj = bias_fn(h, j)
            out.append(sj if bj is None else sj + bj)
        return out

    units = [(h, g) for h in range(N_NA_HEADS) for g in range(len(problems))]
    ahead = ATTN_LOGITS_AHEAD
    queue = [logits(u) for u in units[:ahead]]
    even_out = [None] * len(problems)
    for n, (h, g) in enumerate(units):
        s = queue.pop(0)
        if n + ahead < len(units):
            queue.append(logits(units[n + ahead]))
        _, _, value_refs, _, o_ref = problems[g]
        m = s[0]
        for sj in s[1:]:
            m = jnp.maximum(m, sj)
        m = jnp.max(m, axis=-1, keepdims=True)
        p = jnp.concatenate([jnp.exp2(sj - m).astype(BF16) for sj in s], axis=1)
        v = jnp.concatenate([v_ref[:, lanes(h)] for v_ref in value_refs], axis=0)
        v_aug = jnp.where(own_lanes(v.shape, h), v, jnp.ones_like(v))
        acc = _dot(p, v_aug)
        denom = pltpu.roll(acc, HEAD_DIM, 1)
        out = acc * (1.0 / denom)
        if h % 2 == 0:
            even_out[g] = out
        else:
            o_ref[:, lanes(h)] = jnp.where(own_lanes(out.shape, 0), even_out[g], out).astype(o_ref.dtype)


def _natten_kernel(q_ref, *refs):
    nb, ng, tq = ATTN_KEY_BLOCKS, ATTN_BLOCKS_PER_STEP, ATTN_Q
    kv_refs, (kc_ref, vc_ref), bias_refs, o_ref = (
        refs[:2 * nb * ng], refs[2 * nb * ng:2 * nb * ng + 2], refs[2 * nb * ng + 2:-1], refs[-1])

    def problem(g):
        k_refs = kv_refs[2 * nb * g:2 * nb * g + nb]
        v_refs = kv_refs[2 * nb * g + nb:2 * nb * (g + 1)]
        bias_ref = bias_refs[g]

        def bias_fn(h, j):
            return bias_ref[h, :, j * tq:(j + 1) * tq] if j < nb else None

        rows = pl.ds(g * tq, tq)
        return (q_ref.at[rows], list(k_refs) + [kc_ref], list(v_refs) + [vc_ref], bias_fn, o_ref.at[rows])

    _attend_heads([problem(g) for g in range(ng)])


def _neighbourhood_attention(qkv, kv_ctx, ctx_k_col, bias_tables, layer, batch, seq, ctx_len):
    m = qkv.shape[0]
    tq, ng = ATTN_Q, ATTN_BLOCKS_PER_STEP
    nrb = seq // tq
    assert ctx_len == tq, "context keys are processed as one key block"

    def q_map(b, r):
        return (b * (nrb // ng) + r, 0)

    def kv_map(g, j, col):
        return lambda b, r: (b * nrb + _attn_key_block_start(ng * r + g, nrb) + j, col)

    def bias_map(g):
        def index(b, r):
            rb = ng * r + g
            return (layer, jnp.where(rb == 0, 0, jnp.where(rb == nrb - 1, 2, 1)), 0, 0, 0)
        return index

    blk = (tq, NA_WIDTH)
    in_specs = [pl.BlockSpec((ng * tq, NA_WIDTH), q_map)]
    for g in range(ng):
        in_specs += [pl.BlockSpec(blk, kv_map(g, j, 1)) for j in range(ATTN_KEY_BLOCKS)]
        in_specs += [pl.BlockSpec(blk, kv_map(g, j, 2)) for j in range(ATTN_KEY_BLOCKS)]
    in_specs += [
        pl.BlockSpec((ctx_len, NA_WIDTH), lambda b, r: (b, ctx_k_col)),
        pl.BlockSpec((ctx_len, NA_WIDTH), lambda b, r: (b, ctx_k_col + 1)),
    ]
    in_specs += [pl.BlockSpec((None, None, N_NA_HEADS, tq, ATTN_KEY_BLOCKS * tq), bias_map(g))
                 for g in range(ng)]
    args = [qkv] * (1 + 2 * ATTN_KEY_BLOCKS * ng) + [kv_ctx, kv_ctx] + [bias_tables] * ng
    return pl.pallas_call(
        _natten_kernel,
        grid=(batch, nrb // ng),
        in_specs=in_specs,
        out_specs=pl.BlockSpec((ng * tq, NA_WIDTH), q_map),
        out_shape=jax.ShapeDtypeStruct((m, NA_WIDTH), BF16),
        compiler_params=_params("parallel", "arbitrary"),
        name="neighbourhood_attention",
    )(*args)


def _ctx_attn_kernel(q_ref, k_ref, v_ref, o_ref):
    _attend_heads([(q_ref, [k_ref], [v_ref], lambda h, j: None, o_ref)])


def _context_attention(qkv, batch, ctx_len):
    blk = (ctx_len, NA_WIDTH)
    return pl.pallas_call(
        _ctx_attn_kernel,
        grid=(batch,),
        in_specs=[pl.BlockSpec(blk, lambda b: (b, 0)),
                  pl.BlockSpec(blk, lambda b: (b, 1)),
                  pl.BlockSpec(blk, lambda b: (b, 2))],
        out_specs=pl.BlockSpec(blk, lambda b: (b, 0)),
        out_shape=jax.ShapeDtypeStruct((batch * ctx_len, NA_WIDTH), BF16),
        compiler_params=_params("parallel"),
        name="context_attention",
    )(qkv, qkv, qkv)


HALO_ROWS = 16


def _merge_mlp_kernel(x_ref, mod_ref, f_ref, cv_ref, cvp_ref, cvn_ref, at_ref, gt_ref, cw_ref,
                      wf_ref, wc_ref, wa_ref, wo_ref, g2_ref, w1_ref, w2_ref, fg_ref, o_ref,
                      cv_scr, x1_scr, *, tiles_per_seq, final_norm):
    i = pl.program_id(0)
    tm = x_ref.shape[0]
    cw = CONV_WIDTH
    has_prev = (i % tiles_per_seq) != 0
    has_next = (i % tiles_per_seq) != tiles_per_seq - 1

    u = cv_ref[:, 0:cw].astype(F32)
    gb = cv_ref[:, cw:2 * cw].astype(F32)
    gc = cv_ref[:, 2 * cw:3 * cw].astype(F32)
    z = gc * u
    zp_row = (cvp_ref[HALO_ROWS - 1:HALO_ROWS, 2 * cw:3 * cw].astype(F32)
              * cvp_ref[HALO_ROWS - 1:HALO_ROWS, 0:cw].astype(F32))
    zn_row = cvn_ref[0:1, 2 * cw:3 * cw].astype(F32) * cvn_ref[0:1, 0:cw].astype(F32)
    zp_row = jnp.where(has_prev, zp_row, 0.0)
    zn_row = jnp.where(has_next, zn_row, 0.0)
    row = lax.broadcasted_iota(jnp.int32, (tm, cw), 0)
    z_prev = jnp.where(row == 0, zp_row, pltpu.roll(z, 1, 0))
    z_next = jnp.where(row == tm - 1, zn_row, pltpu.roll(z, tm - 1, 0))
    y = cw_ref[0:1, :] * z_prev + cw_ref[1:2, :] * z + cw_ref[2:3, :] * z_next
    cv_scr[...] = (gb * y).astype(BF16)

    d = D_MODEL
    gate = mod_ref[0, 2:3, :]

    def branches(c):
        r = slice(c * MERGE_ROW_CHUNK, (c + 1) * MERGE_ROW_CHUNK)
        return (_dot(f_ref[r, :].astype(BF16), wf_ref[...]), _dot(cv_scr[r, :], wc_ref[...]),
                _dot(at_ref[r, :], wa_ref[...]))

    nxt = branches(0)
    n_chunks = tm // MERGE_ROW_CHUNK
    for c in range(n_chunks):
        r = slice(c * MERGE_ROW_CHUNK, (c + 1) * MERGE_ROW_CHUNK)
        pf, pc, pa = nxt
        if c + 1 < n_chunks:
            nxt = branches(c + 1)
        mix = _twice_sigmoid(gt_ref[r, 0:d]) * pf
        mix = mix + _twice_sigmoid(gt_ref[r, d:2 * d]) * pc
        mix = mix + _twice_sigmoid(gt_ref[r, 2 * d:3 * d]) * pa
        proj = _dot(mix.astype(BF16), wo_ref[...])
        x1_scr[r, :] = x_ref[r, :] + (0.5 * gate) * proj

    x1 = x1_scr[...]
    h = _norm_modulate(x1, g2_ref[...], mod_ref[0, 3:4, :], mod_ref[0, 4:5, :]).astype(BF16)
    acc = None
    for c0 in range(0, MLP_HIDDEN, MLP_HIDDEN_CHUNK):
        a = jnp.maximum(_dot(h, w1_ref[:, c0:c0 + MLP_HIDDEN_CHUNK]), 0.0)
        part = _dot((a * a).astype(BF16), w2_ref[c0:c0 + MLP_HIDDEN_CHUNK, :])
        acc = part if acc is None else acc + part
    y = x1 + mod_ref[0, 5:6, :] * acc
    if final_norm:
        ms = jnp.mean(y * y, axis=-1, keepdims=True)
        y = y * lax.rsqrt(ms + EPS) * fg_ref[...]
    o_ref[...] = y


def _merge_mlp(x2d, mod, f, conv, attn, gates, conv_w, w_f, w_c, w_a, w_o, g2, w1, w2, final_g,
               seq, tm, final_norm):
    m, d = x2d.shape
    tiles_per_seq = seq // tm
    halo_per_tile = tm // HALO_ROWS
    n_halo = m // HALO_ROWS
    kern = functools.partial(_merge_mlp_kernel, tiles_per_seq=tiles_per_seq, final_norm=final_norm)
    const = lambda i: (0, 0)
    resident = dict(pipeline_mode=pl.Buffered(1))
    return pl.pallas_call(
        kern,
        grid=(m // tm,),
        in_specs=[
            pl.BlockSpec((tm, d), lambda i: (i, 0)),
            pl.BlockSpec((1, N_MOD, d), lambda i: (i // tiles_per_seq, 0, 0)),
            pl.BlockSpec((tm, FOURIER_WIDTH), lambda i: (i, 0)),
            pl.BlockSpec((tm, 3 * CONV_WIDTH), lambda i: (i, 0)),
            pl.BlockSpec((HALO_ROWS, 3 * CONV_WIDTH),
                         lambda i: (jnp.maximum(i * halo_per_tile - 1, 0), 0)),
            pl.BlockSpec((HALO_ROWS, 3 * CONV_WIDTH),
                         lambda i: (jnp.minimum((i + 1) * halo_per_tile, n_halo - 1), 0)),
            pl.BlockSpec((tm, NA_WIDTH), lambda i: (i, 0)),
            pl.BlockSpec((tm, 3 * d), lambda i: (i, 0)),
            pl.BlockSpec((CONV_K, CONV_WIDTH), const),
            pl.BlockSpec((FOURIER_WIDTH, d), const, **resident),
            pl.BlockSpec((CONV_WIDTH, d), const, **resident),
            pl.BlockSpec((NA_WIDTH, d), const, **resident),
            pl.BlockSpec((d, d), const, **resident),
            pl.BlockSpec((1, d), const),
            pl.BlockSpec((d, MLP_HIDDEN), const, **resident),
            pl.BlockSpec((MLP_HIDDEN, d), const, **resident),
            pl.BlockSpec((1, d), const),
        ],
        out_specs=pl.BlockSpec((tm, d), lambda i: (i, 0)),
        out_shape=jax.ShapeDtypeStruct((m, d), F32),
        scratch_shapes=[pltpu.VMEM((tm, CONV_WIDTH), BF16), pltpu.VMEM((tm, d), F32)],
        compiler_params=_params("parallel"),
        name="merge_mlp",
    )(x2d, mod, f, conv, conv, conv, attn, gates, conv_w, w_f, w_c, w_a, w_o, g2, w1, w2, final_g)


ALL_SEGMENTS = (SEG_FOURIER, SEG_CONV, SEG_QKV, SEG_GATES)
SEGMENT_DTYPES = (F32, BF16, BF16, BF16)


def kernel(x, c, ctx, c_ctx, ada_w, ada_b, norm1_g, norm2_g, w_in, conv_w, rel_bias,
           w_fourier, w_conv, w_attn, w_o, mlp_w1, mlp_w2, final_g):
    batch, seq, d = x.shape
    ctx_len = ctx.shape[1]
    rows = seq // GRID_W
    depth = ada_w.shape[0]

    n_cond = batch + 1
    pad = (-n_cond) % 8
    cond = jnp.concatenate([c, c_ctx[None, :], jnp.zeros((pad, d), F32)], axis=0)
    mod_all = _modulation(cond, ada_w, ada_b)

    x2 = x.reshape(batch * seq, d)
    c2 = ctx.reshape(batch * ctx_len, d)
    fg = final_g.reshape(1, d)
    bias_tables = _attn_bias_table(rel_bias, rows)

    for l in range(depth):
        last = l == depth - 1
        mod_x = mod_all[l, :batch].reshape(batch, N_MOD, d)
        mod_c = jnp.broadcast_to(mod_all[l, batch].reshape(1, N_MOD, d), (batch, N_MOD, d))
        g1 = norm1_g[l].reshape(1, d)
        g2 = norm2_g[l].reshape(1, d)
        w_in_b = w_in[l].astype(BF16)
        w_f_b = w_fourier[l].astype(BF16)
        w_c_b = w_conv[l].astype(BF16)
        w_a_b = w_attn[l].astype(BF16)
        w_o_b = w_o[l].astype(BF16)
        w1_b = mlp_w1[l].astype(BF16)
        w2_b = mlp_w2[l].astype(BF16)

        if last:
            (kv_c,) = _in_projection(c2, mod_c, g1, w_in_b[:, KV_START:KV_END],
                                     ((0, KV_END - KV_START),), (BF16,), ctx_len, ctx_len, 0, 1)
            ctx_k_col = 0
        else:
            uf_c, conv_c, qkv_c, gates_c = _in_projection(
                c2, mod_c, g1, w_in_b, ALL_SEGMENTS, SEGMENT_DTYPES, ctx_len, ctx_len, 0, 1)
            kv_c, ctx_k_col = qkv_c, 1

        uf_x, conv_x, qkv_x, gates_x = _in_projection(
            x2, mod_x, g1, w_in_b, ALL_SEGMENTS, SEGMENT_DTYPES, seq, TM_INPROJ, 0, 1)
        f_x = _fourier_latent(uf_x, batch, seq)
        at_x = _neighbourhood_attention(qkv_x, kv_c, ctx_k_col, bias_tables, l, batch, seq, ctx_len)
        x2 = _merge_mlp(x2, mod_x, f_x, conv_x, at_x, gates_x, conv_w[l], w_f_b, w_c_b, w_a_b, w_o_b,
                        g2, w1_b, w2_b, fg, seq, TM_LATENT, final_norm=last)
        if not last:
            f_c = _fourier_context(uf_c, batch, ctx_len)
            at_c = _context_attention(qkv_c, batch, ctx_len)
            c2 = _merge_mlp(c2, mod_c, f_c, conv_c, at_c, gates_c, conv_w[l], w_f_b, w_c_b, w_a_b, w_o_b,
                            g2, w1_b, w2_b, fg, ctx_len, ctx_len, final_norm=False)

    return x2.reshape(batch, seq, d)
```

```python
import functools
import math

import numpy as np
import jax
import jax.numpy as jnp
from jax import lax
from jax.experimental import pallas as pl
from jax.experimental.pallas import tpu as pltpu

D_MODEL = 1024
DEPTH = 2
GRID_W = 64
HEAD_DIM = 64
N_NA_HEADS = 8
NA_WIDTH = N_NA_HEADS * HEAD_DIM
N_FOURIER_GROUPS = 4
FOURIER_WIDTH = D_MODEL // 4
FOURIER_GROUP = FOURIER_WIDTH // N_FOURIER_GROUPS
CONV_WIDTH = D_MODEL // 4
CONV_K = 3
WIN_ROWS = 8
WIN_COLS = 16
MLP_HIDDEN = 4 * D_MODEL
N_MOD = 6
EPS = 1e-6
NEG = -1e30
LOG2E = math.log2(math.e)

SEG_FOURIER = (0, FOURIER_WIDTH)
SEG_CONV = (SEG_FOURIER[1], SEG_FOURIER[1] + 3 * CONV_WIDTH)
SEG_QKV = (SEG_CONV[1], SEG_CONV[1] + 3 * NA_WIDTH)
SEG_GATES = (SEG_QKV[1], SEG_QKV[1] + 3 * D_MODEL)
KV_START = SEG_QKV[0] + NA_WIDTH
KV_END = SEG_QKV[1]

LANES = 128
VMEM_LIMIT_BYTES = 56 * 1024 * 1024

TM_LATENT = 512
TM_INPROJ = 1024
ROWS_PER_ATTN_STEP = 4
ATTN_Q = ROWS_PER_ATTN_STEP * GRID_W
ATTN_KEY_BLOCKS = 3
ATTN_LOGITS_AHEAD = 1
ATTN_BLOCKS_PER_STEP = 1
DFT_N1 = 64
DFT_ROWS_PER_GROUP = 8
DFT_STAGE1_GROUPS = 4
DFT_STAGE2_GROUPS = 2
MOD_COL_TILE = 1536
PROJ_COL_CHUNK = 512
MLP_HIDDEN_CHUNK = 1024
MERGE_ROW_CHUNK = 256

BF16 = jnp.bfloat16
F32 = jnp.float32


def _params(*semantics):
    return pltpu.CompilerParams(dimension_semantics=semantics, vmem_limit_bytes=VMEM_LIMIT_BYTES)


def _dot(a, b):
    return jnp.dot(a, b, preferred_element_type=F32)


def _dot_nt(a, b):
    return lax.dot_general(a, b, (((1,), (1,)), ((), ())), preferred_element_type=F32)


def _channel_dft_tables():
    c = np.arange(FOURIER_GROUP)
    ang = 2.0 * np.pi * ((c[:, None] * c[None, :]) % FOURIER_GROUP) / FOURIER_GROUP
    eye = np.eye(N_FOURIER_GROUPS)
    s = 1.0 / math.sqrt(FOURIER_GROUP)
    return np.stack([np.kron(eye, np.cos(ang) * s), np.kron(eye, np.sin(ang) * s)]).astype(np.float32)


def _latent_dft_tables(n):
    n1, n2 = DFT_N1, n // DFT_N1
    a = np.arange(n1)
    ang1 = 2.0 * np.pi * ((a[:, None] * a[None, :]) % n1) / n1
    s1 = 1.0 / math.sqrt(n1)
    f1 = np.concatenate([np.cos(ang1) * s1, -np.sin(ang1) * s1], axis=0)
    k1 = np.arange(n1)[:, None, None]
    k2 = np.arange(n2)[None, :, None]
    m2 = np.arange(n2)[None, None, :]
    ang = 2.0 * np.pi * ((m2 * (k1 + n1 * k2)) % n) / n
    s2 = 1.0 / math.sqrt(n2)
    tr, ti = np.cos(ang) * s2, -np.sin(ang) * s2
    g = np.concatenate([np.concatenate([tr, -ti], axis=2),
                        np.concatenate([ti, tr], axis=2)], axis=1)
    return f1.astype(np.float32), g.astype(np.float32)


def _context_dft_table(n):
    a = np.arange(n)
    ang = 2.0 * np.pi * ((a[:, None] * a[None, :]) % n) / n
    s = 1.0 / math.sqrt(n)
    return np.concatenate([np.cos(ang) * s, -np.sin(ang) * s], axis=0).astype(np.float32)


def _attn_key_block_start(rb, n_row_blocks):
    lead_blocks = (WIN_ROWS // 2) // ROWS_PER_ATTN_STEP
    return jnp.clip(rb - lead_blocks, 0, n_row_blocks - ATTN_KEY_BLOCKS)


def _attn_bias_table(rel_bias, rows):
    depth, h, ndr, _ = rel_bias.shape
    w = GRID_W
    rq, nkr = ROWS_PER_ATTN_STEP, ROWS_PER_ATTN_STEP * ATTN_KEY_BLOCKS
    ends = w - WIN_COLS
    ext = jnp.concatenate([
        jnp.broadcast_to(rel_bias[..., :1], (depth, h, ndr, ends)), rel_bias,
        jnp.broadcast_to(rel_bias[..., -1:], (depth, h, ndr, ends)), jnp.zeros((depth, h, ndr, 1), F32)], axis=-1)
    return pl.pallas_call(
        functools.partial(_bias_table_kernel, rows=rows),
        grid=(depth, h),
        in_specs=[pl.BlockSpec((None, None, ndr, 2 * w), lambda l, hh: (l, hh, 0, 0))],
        out_specs=pl.BlockSpec((None, 3, None, rq * w, nkr * w), lambda l, hh: (l, 0, hh, 0, 0)),
        out_shape=jax.ShapeDtypeStruct((depth, 3, h, rq * w, nkr * w), F32),
        compiler_params=_params("parallel", "parallel"),
        name="attn_bias_table",
    )(ext)


def _bias_table_kernel(ext_ref, o_ref, *, rows):
    w = GRID_W
    rq, nkr = ROWS_PER_ATTN_STEP, ROWS_PER_ATTN_STEP * ATTN_KEY_BLOCKS
    q = lax.broadcasted_iota(jnp.int32, (w, 2 * w), 0)
    t = lax.broadcasted_iota(jnp.int32, (w, 2 * w), 1)
    first_col = jnp.clip(q - WIN_COLS // 2, 0, w - WIN_COLS)
    neg = jnp.full((w, 2 * w), NEG, F32)
    left, right = [], []
    for d in range(2 * WIN_ROWS - 1):
        row = jnp.broadcast_to(ext_ref[d:d + 1, :] * LOG2E, (w, 2 * w))
        lo = pltpu.roll(row, w + 1, 1, stride=1, stride_axis=0)
        hi = pltpu.roll(row, 1, 1, stride=1, stride_axis=0)
        left.append(jnp.where((t >= first_col) & (t < first_col + WIN_COLS), lo, neg))
        right.append(jnp.where((t - w >= first_col) & (t - w < first_col + WIN_COLS), hi, neg))

    variants = [(0, 0), (rq, 0), (rows - rq, rows - nkr)]
    for v, (r0, ks) in enumerate(variants):
        for qi in range(rq):
            qr = r0 + qi
            rs = min(max(qr - WIN_ROWS // 2, 0), rows - WIN_ROWS)
            lead = rs - ks
            d0 = rs - qr + WIN_ROWS - 1

            def slab(kj, side):
                return side[d0 + kj - lead] if lead <= kj < lead + WIN_ROWS else neg

            for p in range(nkr // 2):
                tile = jnp.where(t < w, slab(2 * p, left), slab(2 * p + 1, right))
                o_ref[v, qi * w:(qi + 1) * w, p * 2 * w:(p + 1) * 2 * w] = tile


def _mod_kernel(c_ref, w_ref, b_ref, o_ref):
    c = c_ref[...]
    s = c * (1.0 / (1.0 + jnp.exp(-c)))
    o_ref[...] = jnp.dot(s, w_ref[...], precision=lax.Precision.HIGHEST,
                         preferred_element_type=F32) + b_ref[...]


def _modulation(cond, ada_w, ada_b):
    r, d = cond.shape
    depth, _, n = ada_w.shape
    tn = MOD_COL_TILE
    return pl.pallas_call(
        _mod_kernel,
        grid=(depth, n // tn),
        in_specs=[
            pl.BlockSpec((r, d), lambda l, j: (0, 0)),
            pl.BlockSpec((None, d, tn), lambda l, j: (l, 0, j)),
            pl.BlockSpec((None, 1, tn), lambda l, j: (l, 0, j)),
        ],
        out_specs=pl.BlockSpec((None, r, tn), lambda l, j: (l, 0, j)),
        out_shape=jax.ShapeDtypeStruct((depth, r, n), F32),
        compiler_params=_params("parallel", "parallel"),
        name="adaln_modulation",
    )(cond, ada_w, ada_b.reshape(depth, 1, n))


def _norm_modulate(x, g, shift, scale):
    ms = jnp.mean(x * x, axis=-1, keepdims=True)
    return (x * lax.rsqrt(ms + EPS)) * (g * (1.0 + scale)) + shift


def _twice_sigmoid(g_bf16):
    return jnp.tanh((g_bf16 * 0.5).astype(F32)) + 1.0


def _inproj_kernel(x_ref, mod_ref, g_ref, w_ref, *out_refs, segments, shift_row, scale_row):
    h = _norm_modulate(x_ref[...], g_ref[...],
                       mod_ref[0, shift_row:shift_row + 1, :], mod_ref[0, scale_row:scale_row + 1, :])
    hb = h.astype(BF16)
    for o_ref, (lo, hi) in zip(out_refs, segments):
        for c0 in range(lo, hi, PROJ_COL_CHUNK):
            c1 = min(c0 + PROJ_COL_CHUNK, hi)
            o_ref[:, c0 - lo:c1 - lo] = _dot(hb, w_ref[:, c0:c1]).astype(o_ref.dtype)


def _resident_weight_spec(w, layer):
    return pl.BlockSpec((None,) + w.shape[1:], lambda *_: (layer, 0, 0), pipeline_mode=pl.Buffered(1))


def _in_projection(x2d, mod, g, w_bf16, layer, segments, seg_dtypes, seq, tm, shift_row, scale_row):
    m, d = x2d.shape
    tiles_per_seq = seq // tm
    kern = functools.partial(_inproj_kernel, segments=tuple(segments),
                             shift_row=shift_row, scale_row=scale_row)
    return pl.pallas_call(
        kern,
        grid=(m // tm,),
        in_specs=[
            pl.BlockSpec((tm, d), lambda i: (i, 0)),
            pl.BlockSpec((1, N_MOD, d), lambda i: (i // tiles_per_seq, 0, 0)),
            pl.BlockSpec((1, d), lambda i: (0, 0)),
            _resident_weight_spec(w_bf16, layer),
        ],
        out_specs=[pl.BlockSpec((tm, hi - lo), lambda i: (i, 0)) for lo, hi in segments],
        out_shape=[jax.ShapeDtypeStruct((m, hi - lo), dt) for (lo, hi), dt in zip(segments, seg_dtypes)],
        compiler_params=_params("parallel"),
        name="norm_inproj",
    )(x2d, mod, g, w_bf16)


def _dft_stage1_kernel(x_ref, f_ref, o_ref):
    n1, rows, c = x_ref.shape
    t = DFT_ROWS_PER_GROUP
    f = f_ref[...].astype(BF16)
    for g in range(rows // t):
        x = x_ref[:, g * t:(g + 1) * t, :].reshape(n1 * t, c).astype(BF16)
        o_ref[:, :, g * t:(g + 1) * t, :] = _dot(f, x).reshape(2, n1, t, c)


def _dft_stage2_kernel(z_ref, g_ref, cs_ref, o_ref, *, n2):
    cos_t, sin_t = cs_ref[0].astype(BF16), cs_ref[1].astype(BF16)
    c = z_ref.shape[3]
    for j in range(z_ref.shape[1]):
        zz = z_ref[:, j].reshape(2 * n2, c).astype(BF16)
        p = _dot(g_ref[j].astype(BF16), zz).astype(BF16)
        o_ref[:, j, :] = _dot(p[:n2], cos_t) + _dot(p[n2:], sin_t)


def _fourier_latent(u, batch, seq):
    c = u.shape[1]
    n1, n2 = DFT_N1, seq // DFT_N1
    f1_np, g_np = _latent_dft_tables(seq)
    t = DFT_ROWS_PER_GROUP
    r1, r2 = t * DFT_STAGE1_GROUPS, t * DFT_STAGE2_GROUPS
    f1 = jnp.asarray(np.kron(f1_np, np.eye(t, dtype=np.float32)))
    g = jnp.asarray(g_np)
    cs = jnp.asarray(_channel_dft_tables())
    z = pl.pallas_call(
        _dft_stage1_kernel,
        grid=(batch, n2 // r1),
        in_specs=[
            pl.BlockSpec((None, n1, r1, c), lambda b, j: (b, 0, j, 0)),
            pl.BlockSpec((2 * n1 * t, n1 * t), lambda b, j: (0, 0)),
        ],
        out_specs=pl.BlockSpec((None, 2, n1, r1, c), lambda b, j: (b, 0, 0, j, 0)),
        out_shape=jax.ShapeDtypeStruct((batch, 2, n1, n2, c), F32),
        compiler_params=_params("parallel", "parallel"),
        name="dft_stage1",
    )(u.reshape(batch, n1, n2, c), f1)
    y = pl.pallas_call(
        functools.partial(_dft_stage2_kernel, n2=n2),
        grid=(n1 // r2, batch),
        in_specs=[
            pl.BlockSpec((None, 2, r2, n2, c), lambda k, b: (b, 0, k, 0, 0)),
            pl.BlockSpec((r2, 2 * n2, 2 * n2), lambda k, b: (k, 0, 0)),
            pl.BlockSpec((2, c, c), lambda k, b: (0, 0, 0)),
        ],
        out_specs=pl.BlockSpec((None, n2, r2, c), lambda k, b: (b, 0, k, 0)),
        out_shape=jax.ShapeDtypeStruct((batch, n2, n1, c), F32),
        compiler_params=_params("parallel", "parallel"),
        name="dft_stage2",
    )(z, g, cs)
    return y.reshape(batch * seq, c)


def _dft_context_kernel(u_ref, f_ref, cs_ref, o_ref, *, n):
    p = _dot(f_ref[...].astype(BF16), u_ref[...].astype(BF16)).astype(BF16)
    y = _dot(p[:n], cs_ref[0].astype(BF16)) + _dot(p[n:], cs_ref[1].astype(BF16))
    o_ref[...] = y.astype(o_ref.dtype)


def _fourier_context(u, batch, seq):
    c = u.shape[1]
    f = jnp.asarray(_context_dft_table(seq))
    cs = jnp.asarray(_channel_dft_tables())
    return pl.pallas_call(
        functools.partial(_dft_context_kernel, n=seq),
        grid=(batch,),
        in_specs=[
            pl.BlockSpec((seq, c), lambda b: (b, 0)),
            pl.BlockSpec((2 * seq, seq), lambda b: (0, 0)),
            pl.BlockSpec((2, c, c), lambda b: (0, 0, 0)),
        ],
        out_specs=pl.BlockSpec((seq, c), lambda b: (b, 0)),
        out_shape=jax.ShapeDtypeStruct((batch * seq, c), BF16),
        compiler_params=_params("parallel"),
        name="dft_context",
    )(u, f, cs)


def _attend_heads(problems):
    q_scale = HEAD_DIM ** -0.5 * LOG2E

    def lanes(h):
        return slice((h // 2) * LANES, (h // 2 + 1) * LANES)

    def own_lanes(shape, h):
        low = lax.broadcasted_iota(jnp.int32, shape, 1) < HEAD_DIM
        return low if h % 2 == 0 else jnp.logical_not(low)

    def logits(unit):
        h, g = unit
        q_ref, key_refs, _, bias_fn, _ = problems[g]
        q_pair = q_ref[:, lanes(h)]
        qs = (q_pair.astype(F32) * q_scale).astype(BF16)
        qm = jnp.where(own_lanes(qs.shape, h), qs, jnp.zeros_like(qs))
        out = []
        for j, k_ref in enumerate(key_refs):
            sj = _dot_nt(qm, k_ref[:, lanes(h)])
            bj = bias_fn(h, j)
            out.append(sj if bj is None else sj + bj)
        return out

    units = [(h, g) for h in range(N_NA_HEADS) for g in range(len(problems))]
    ahead = ATTN_LOGITS_AHEAD
    queue = [logits(u) for u in units[:ahead]]
    even_out = [None] * len(problems)
    for n, (h, g) in enumerate(units):
        s = queue.pop(0)
        if n + ahead < len(units):
            queue.append(logits(units[n + ahead]))
        _, _, value_refs, _, o_ref = problems[g]
        m = s[0]
        for sj in s[1:]:
            m = jnp.maximum(m, sj)
        m = jnp.max(m, axis=-1, keepdims=True)
        p = jnp.concatenate([jnp.exp2(sj - m).astype(BF16) for sj in s], axis=1)
        v = jnp.concatenate([v_ref[:, lanes(h)] for v_ref in value_refs], axis=0)
        v_aug = jnp.where(own_lanes(v.shape, h), v, jnp.ones_like(v))
        acc = _dot(p, v_aug)
        denom = pltpu.roll(acc, HEAD_DIM, 1)
        out = acc * (1.0 / denom)
        if h % 2 == 0:
            even_out[g] = out
        else:
            o_ref[:, lanes(h)] = jnp.where(own_lanes(out.shape, 0), even_out[g], out).astype(o_ref.dtype)


def _natten_kernel(q_ref, *refs):
    nb, ng, tq = ATTN_KEY_BLOCKS, ATTN_BLOCKS_PER_STEP, ATTN_Q
    kv_refs, (kc_ref, vc_ref), bias_refs, o_ref = (
        refs[:2 * nb * ng], refs[2 * nb * ng:2 * nb * ng + 2], refs[2 * nb * ng + 2:-1], refs[-1])

    def problem(g):
        k_refs = kv_refs[2 * nb * g:2 * nb * g + nb]
        v_refs = kv_refs[2 * nb * g + nb:2 * nb * (g + 1)]
        bias_ref = bias_refs[g]

        def bias_fn(h, j):
            return bias_ref[h, :, j * tq:(j + 1) * tq] if j < nb else None

        rows = pl.ds(g * tq, tq)
        return (q_ref.at[rows], list(k_refs) + [kc_ref], list(v_refs) + [vc_ref], bias_fn, o_ref.at[rows])

    _attend_heads([problem(g) for g in range(ng)])


def _neighbourhood_attention(qkv, kv_ctx, ctx_k_col, bias_tables, layer, batch, seq, ctx_len):
    m = qkv.shape[0]
    tq, ng = ATTN_Q, ATTN_BLOCKS_PER_STEP
    nrb = seq // tq
    assert ctx_len == tq, "context keys are processed as one key block"

    def q_map(b, r):
        return (b * (nrb // ng) + r, 0)

    def kv_map(g, j, col):
        return lambda b, r: (b * nrb + _attn_key_block_start(ng * r + g, nrb) + j, col)

    def bias_map(g):
        def index(b, r):
            rb = ng * r + g
            return (layer, jnp.where(rb == 0, 0, jnp.where(rb == nrb - 1, 2, 1)), 0, 0, 0)
        return index

    blk = (tq, NA_WIDTH)
    in_specs = [pl.BlockSpec((ng * tq, NA_WIDTH), q_map)]
    for g in range(ng):
        in_specs += [pl.BlockSpec(blk, kv_map(g, j, 1)) for j in range(ATTN_KEY_BLOCKS)]
        in_specs += [pl.BlockSpec(blk, kv_map(g, j, 2)) for j in range(ATTN_KEY_BLOCKS)]
    in_specs += [
        pl.BlockSpec((ctx_len, NA_WIDTH), lambda b, r: (b, ctx_k_col)),
        pl.BlockSpec((ctx_len, NA_WIDTH), lambda b, r: (b, ctx_k_col + 1)),
    ]
    in_specs += [pl.BlockSpec((None, None, N_NA_HEADS, tq, ATTN_KEY_BLOCKS * tq), bias_map(g))
                 for g in range(ng)]
    args = [qkv] * (1 + 2 * ATTN_KEY_BLOCKS * ng) + [kv_ctx, kv_ctx] + [bias_tables] * ng
    return pl.pallas_call(
        _natten_kernel,
        grid=(batch, nrb // ng),
        in_specs=in_specs,
        out_specs=pl.BlockSpec((ng * tq, NA_WIDTH), q_map),
        out_shape=jax.ShapeDtypeStruct((m, NA_WIDTH), BF16),
        compiler_params=_params("parallel", "arbitrary"),
        name="neighbourhood_attention",
    )(*args)


def _ctx_attn_kernel(q_ref, k_ref, v_ref, o_ref):
    _attend_heads([(q_ref, [k_ref], [v_ref], lambda h, j: None, o_ref)])


def _context_attention(qkv, batch, ctx_len):
    blk = (ctx_len, NA_WIDTH)
    return pl.pallas_call(
        _ctx_attn_kernel,
        grid=(batch,),
        in_specs=[pl.BlockSpec(blk, lambda b: (b, 0)),
                  pl.BlockSpec(blk, lambda b: (b, 1)),
                  pl.BlockSpec(blk, lambda b: (b, 2))],
        out_specs=pl.BlockSpec(blk, lambda b: (b, 0)),
        out_shape=jax.ShapeDtypeStruct((batch * ctx_len, NA_WIDTH), BF16),
        compiler_params=_params("parallel"),
        name="context_attention",
    )(qkv, qkv, qkv)


HALO_ROWS = 16


def _merge_mlp_kernel(x_ref, mod_ref, f_ref, cv_ref, cvp_ref, cvn_ref, at_ref, gt_ref, cw_ref,
                      wf_ref, wc_ref, wa_ref, wo_ref, g2_ref, w1_ref, w2_ref, fg_ref, o_ref,
                      cv_scr, x1_scr, *, tiles_per_seq, final_norm):
    i = pl.program_id(0)
    tm = x_ref.shape[0]
    cw = CONV_WIDTH
    has_prev = (i % tiles_per_seq) != 0
    has_next = (i % tiles_per_seq) != tiles_per_seq - 1

    u = cv_ref[:, 0:cw].astype(F32)
    gb = cv_ref[:, cw:2 * cw].astype(F32)
    gc = cv_ref[:, 2 * cw:3 * cw].astype(F32)
    z = gc * u
    zp_row = (cvp_ref[HALO_ROWS - 1:HALO_ROWS, 2 * cw:3 * cw].astype(F32)
              * cvp_ref[HALO_ROWS - 1:HALO_ROWS, 0:cw].astype(F32))
    zn_row = cvn_ref[0:1, 2 * cw:3 * cw].astype(F32) * cvn_ref[0:1, 0:cw].astype(F32)
    zp_row = jnp.where(has_prev, zp_row, 0.0)
    zn_row = jnp.where(has_next, zn_row, 0.0)
    row = lax.broadcasted_iota(jnp.int32, (tm, cw), 0)
    z_prev = jnp.where(row == 0, zp_row, pltpu.roll(z, 1, 0))
    z_next = jnp.where(row == tm - 1, zn_row, pltpu.roll(z, tm - 1, 0))
    y = cw_ref[0:1, :] * z_prev + cw_ref[1:2, :] * z + cw_ref[2:3, :] * z_next
    cv_scr[...] = (gb * y).astype(BF16)

    d = D_MODEL
    gate = mod_ref[0, 2:3, :]

    def branches(c):
        r = slice(c * MERGE_ROW_CHUNK, (c + 1) * MERGE_ROW_CHUNK)
        return (_dot(f_ref[r, :].astype(BF16), wf_ref[...]), _dot(cv_scr[r, :], wc_ref[...]),
                _dot(at_ref[r, :], wa_ref[...]))

    nxt = branches(0)
    n_chunks = tm // MERGE_ROW_CHUNK
    for c in range(n_chunks):
        r = slice(c * MERGE_ROW_CHUNK, (c + 1) * MERGE_ROW_CHUNK)
        pf, pc, pa = nxt
        if c + 1 < n_chunks:
            nxt = branches(c + 1)
        mix = _twice_sigmoid(gt_ref[r, 0:d]) * pf
        mix = mix + _twice_sigmoid(gt_ref[r, d:2 * d]) * pc
        mix = mix + _twice_sigmoid(gt_ref[r, 2 * d:3 * d]) * pa
        proj = _dot(mix.astype(BF16), wo_ref[...])
        x1_scr[r, :] = x_ref[r, :] + (0.5 * gate) * proj

    x1 = x1_scr[...]
    h = _norm_modulate(x1, g2_ref[...], mod_ref[0, 3:4, :], mod_ref[0, 4:5, :]).astype(BF16)
    acc = None
    for c0 in range(0, MLP_HIDDEN, MLP_HIDDEN_CHUNK):
        a = jnp.maximum(_dot(h, w1_ref[:, c0:c0 + MLP_HIDDEN_CHUNK]), 0.0)
        part = _dot((a * a).astype(BF16), w2_ref[c0:c0 + MLP_HIDDEN_CHUNK, :])
        acc = part if acc is None else acc + part
    y = x1 + mod_ref[0, 5:6, :] * acc
    if final_norm:
        ms = jnp.mean(y * y, axis=-1, keepdims=True)
        y = y * lax.rsqrt(ms + EPS) * fg_ref[...]
    o_ref[...] = y


def _merge_mlp(x2d, mod, f, conv, attn, gates, conv_w, w_f, w_c, w_a, w_o, g2, w1, w2, final_g,
               layer, seq, tm, final_norm):
    m, d = x2d.shape
    tiles_per_seq = seq // tm
    halo_per_tile = tm // HALO_ROWS
    n_halo = m // HALO_ROWS
    kern = functools.partial(_merge_mlp_kernel, tiles_per_seq=tiles_per_seq, final_norm=final_norm)
    const = lambda i: (0, 0)
    return pl.pallas_call(
        kern,
        grid=(m // tm,),
        in_specs=[
            pl.BlockSpec((tm, d), lambda i: (i, 0)),
            pl.BlockSpec((1, N_MOD, d), lambda i: (i // tiles_per_seq, 0, 0)),
            pl.BlockSpec((tm, FOURIER_WIDTH), lambda i: (i, 0)),
            pl.BlockSpec((tm, 3 * CONV_WIDTH), lambda i: (i, 0)),
            pl.BlockSpec((HALO_ROWS, 3 * CONV_WIDTH),
                         lambda i: (jnp.maximum(i * halo_per_tile - 1, 0), 0)),
            pl.BlockSpec((HALO_ROWS, 3 * CONV_WIDTH),
                         lambda i: (jnp.minimum((i + 1) * halo_per_tile, n_halo - 1), 0)),
            pl.BlockSpec((tm, NA_WIDTH), lambda i: (i, 0)),
            pl.BlockSpec((tm, 3 * d), lambda i: (i, 0)),
            pl.BlockSpec((CONV_K, CONV_WIDTH), const),
            _resident_weight_spec(w_f, layer),
            _resident_weight_spec(w_c, layer),
            _resident_weight_spec(w_a, layer),
            _resident_weight_spec(w_o, layer),
            pl.BlockSpec((1, d), const),
            _resident_weight_spec(w1, layer),
            _resident_weight_spec(w2, layer),
            pl.BlockSpec((1, d), const),
        ],
        out_specs=pl.BlockSpec((tm, d), lambda i: (i, 0)),
        out_shape=jax.ShapeDtypeStruct((m, d), F32),
        scratch_shapes=[pltpu.VMEM((tm, CONV_WIDTH), BF16), pltpu.VMEM((tm, d), F32)],
        compiler_params=_params("parallel"),
        name="merge_mlp",
    )(x2d, mod, f, conv, conv, conv, attn, gates, conv_w, w_f, w_c, w_a, w_o, g2, w1, w2, final_g)


ALL_SEGMENTS = (SEG_FOURIER, SEG_CONV, SEG_QKV, SEG_GATES)
SEGMENT_DTYPES = (F32, BF16, BF16, BF16)


def kernel(x, c, ctx, c_ctx, ada_w, ada_b, norm1_g, norm2_g, w_in, conv_w, rel_bias,
           w_fourier, w_conv, w_attn, w_o, mlp_w1, mlp_w2, final_g):
    batch, seq, d = x.shape
    ctx_len = ctx.shape[1]
    rows = seq // GRID_W
    depth = ada_w.shape[0]

    n_cond = batch + 1
    pad = (-n_cond) % 8
    cond = jnp.concatenate([c, c_ctx[None, :], jnp.zeros((pad, d), F32)], axis=0)
    mod_all = _modulation(cond, ada_w, ada_b)

    x2 = x.reshape(batch * seq, d)
    c2 = ctx.reshape(batch * ctx_len, d)
    fg = final_g.reshape(1, d)
    bias_tables = _attn_bias_table(rel_bias, rows)
    w_in_b, w_f_b, w_c_b, w_a_b, w_o_b, w1_b, w2_b = (
        w.astype(BF16) for w in (w_in, w_fourier, w_conv, w_attn, w_o, mlp_w1, mlp_w2))

    for l in range(depth):
        last = l == depth - 1
        mod_x = mod_all[l, :batch].reshape(batch, N_MOD, d)
        mod_c = jnp.broadcast_to(mod_all[l, batch].reshape(1, N_MOD, d), (batch, N_MOD, d))
        g1 = norm1_g[l].reshape(1, d)
        g2 = norm2_g[l].reshape(1, d)

        if last:
            (kv_c,) = _in_projection(c2, mod_c, g1, w_in_b[l:l + 1, :, KV_START:KV_END], 0,
                                     ((0, KV_END - KV_START),), (BF16,), ctx_len, ctx_len, 0, 1)
            ctx_k_col = 0
        else:
            uf_c, conv_c, qkv_c, gates_c = _in_projection(
                c2, mod_c, g1, w_in_b, l, ALL_SEGMENTS, SEGMENT_DTYPES, ctx_len, ctx_len, 0, 1)
            kv_c, ctx_k_col = qkv_c, 1

        uf_x, conv_x, qkv_x, gates_x = _in_projection(
            x2, mod_x, g1, w_in_b, l, ALL_SEGMENTS, SEGMENT_DTYPES, seq, TM_INPROJ, 0, 1)
        f_x = _fourier_latent(uf_x, batch, seq)
        at_x = _neighbourhood_attention(qkv_x, kv_c, ctx_k_col, bias_tables, l, batch, seq, ctx_len)
        x2 = _merge_mlp(x2, mod_x, f_x, conv_x, at_x, gates_x, conv_w[l], w_f_b, w_c_b, w_a_b, w_o_b,
                        g2, w1_b, w2_b, fg, l, seq, TM_LATENT, final_norm=last)
        if not last:
            f_c = _fourier_context(uf_c, batch, ctx_len)
            at_c = _context_attention(qkv_c, batch, ctx_len)
            c2 = _merge_mlp(c2, mod_c, f_c, conv_c, at_c, gates_c, conv_w[l], w_f_b, w_c_b, w_a_b, w_o_b,
                            g2, w1_b, w2_b, fg, l, ctx_len, ctx_len, final_norm=False)

    return x2.reshape(batch, seq, d)
```

```python
import functools
import math

import numpy as np
import jax
import jax.numpy as jnp
from jax import lax
from jax.experimental import pallas as pl
from jax.experimental.pallas import tpu as pltpu

D_MODEL = 1024
DEPTH = 2
GRID_W = 64
HEAD_DIM = 64
N_NA_HEADS = 8
NA_WIDTH = N_NA_HEADS * HEAD_DIM
N_FOURIER_GROUPS = 4
FOURIER_WIDTH = D_MODEL // 4
FOURIER_GROUP = FOURIER_WIDTH // N_FOURIER_GROUPS
CONV_WIDTH = D_MODEL // 4
CONV_K = 3
WIN_ROWS = 8
WIN_COLS = 16
MLP_HIDDEN = 4 * D_MODEL
N_MOD = 6
EPS = 1e-6
NEG = -1e30
LOG2E = math.log2(math.e)

SEG_FOURIER = (0, FOURIER_WIDTH)
SEG_CONV = (SEG_FOURIER[1], SEG_FOURIER[1] + 3 * CONV_WIDTH)
SEG_QKV = (SEG_CONV[1], SEG_CONV[1] + 3 * NA_WIDTH)
SEG_GATES = (SEG_QKV[1], SEG_QKV[1] + 3 * D_MODEL)
KV_START = SEG_QKV[0] + NA_WIDTH
KV_END = SEG_QKV[1]

LANES = 128
VMEM_LIMIT_BYTES = 56 * 1024 * 1024

TM_LATENT = 512
TM_INPROJ = 1024
ROWS_PER_ATTN_STEP = 4
ATTN_Q = ROWS_PER_ATTN_STEP * GRID_W
ATTN_KEY_BLOCKS = 3
ATTN_LOGITS_AHEAD = 1
ATTN_BLOCKS_PER_STEP = 2
DFT_N1 = 64
DFT_ROWS_PER_GROUP = 8
DFT_STAGE1_GROUPS = 8
DFT_STAGE2_GROUPS = 4
MOD_COL_TILE = 1536
PROJ_COL_CHUNK = 512
MLP_HIDDEN_CHUNK = 1024
MERGE_ROW_CHUNK = 256

BF16 = jnp.bfloat16
F32 = jnp.float32


def _params(*semantics):
    return pltpu.CompilerParams(dimension_semantics=semantics, vmem_limit_bytes=VMEM_LIMIT_BYTES)


def _dot(a, b):
    return jnp.dot(a, b, preferred_element_type=F32)


def _dot_nt(a, b):
    return lax.dot_general(a, b, (((1,), (1,)), ((), ())), preferred_element_type=F32)


def _channel_dft_tables():
    c = np.arange(FOURIER_GROUP)
    ang = 2.0 * np.pi * ((c[:, None] * c[None, :]) % FOURIER_GROUP) / FOURIER_GROUP
    eye = np.eye(N_FOURIER_GROUPS)
    s = 1.0 / math.sqrt(FOURIER_GROUP)
    return np.stack([np.kron(eye, np.cos(ang) * s), np.kron(eye, np.sin(ang) * s)]).astype(np.float32)


def _latent_dft_tables(n):
    n1, n2 = DFT_N1, n // DFT_N1
    a = np.arange(n1)
    ang1 = 2.0 * np.pi * ((a[:, None] * a[None, :]) % n1) / n1
    s1 = 1.0 / math.sqrt(n1)
    f1 = np.concatenate([np.cos(ang1) * s1, -np.sin(ang1) * s1], axis=0)
    k1 = np.arange(n1)[:, None, None]
    k2 = np.arange(n2)[None, :, None]
    m2 = np.arange(n2)[None, None, :]
    ang = 2.0 * np.pi * ((m2 * (k1 + n1 * k2)) % n) / n
    s2 = 1.0 / math.sqrt(n2)
    tr, ti = np.cos(ang) * s2, -np.sin(ang) * s2
    g = np.concatenate([np.concatenate([tr, -ti], axis=2),
                        np.concatenate([ti, tr], axis=2)], axis=1)
    return f1.astype(np.float32), g.astype(np.float32)


def _context_dft_table(n):
    a = np.arange(n)
    ang = 2.0 * np.pi * ((a[:, None] * a[None, :]) % n) / n
    s = 1.0 / math.sqrt(n)
    return np.concatenate([np.cos(ang) * s, -np.sin(ang) * s], axis=0).astype(np.float32)


def _attn_key_block_start(rb, n_row_blocks):
    lead_blocks = (WIN_ROWS // 2) // ROWS_PER_ATTN_STEP
    return jnp.clip(rb - lead_blocks, 0, n_row_blocks - ATTN_KEY_BLOCKS)


def _attn_bias_table(rel_bias, rows):
    depth, h, ndr, _ = rel_bias.shape
    w = GRID_W
    rq, nkr = ROWS_PER_ATTN_STEP, ROWS_PER_ATTN_STEP * ATTN_KEY_BLOCKS
    ends = w - WIN_COLS
    ext = jnp.concatenate([
        jnp.broadcast_to(rel_bias[..., :1], (depth, h, ndr, ends)), rel_bias,
        jnp.broadcast_to(rel_bias[..., -1:], (depth, h, ndr, ends)), jnp.zeros((depth, h, ndr, 1), F32)], axis=-1)
    return pl.pallas_call(
        functools.partial(_bias_table_kernel, rows=rows),
        grid=(depth, h),
        in_specs=[pl.BlockSpec((None, None, ndr, 2 * w), lambda l, hh: (l, hh, 0, 0))],
        out_specs=pl.BlockSpec((None, 3, None, rq * w, nkr * w), lambda l, hh: (l, 0, hh, 0, 0)),
        out_shape=jax.ShapeDtypeStruct((depth, 3, h, rq * w, nkr * w), F32),
        compiler_params=_params("parallel", "parallel"),
        name="attn_bias_table",
    )(ext)


def _bias_table_kernel(ext_ref, o_ref, *, rows):
    w = GRID_W
    rq, nkr = ROWS_PER_ATTN_STEP, ROWS_PER_ATTN_STEP * ATTN_KEY_BLOCKS
    q = lax.broadcasted_iota(jnp.int32, (w, 2 * w), 0)
    t = lax.broadcasted_iota(jnp.int32, (w, 2 * w), 1)
    first_col = jnp.clip(q - WIN_COLS // 2, 0, w - WIN_COLS)
    neg = jnp.full((w, 2 * w), NEG, F32)
    left, right = [], []
    for d in range(2 * WIN_ROWS - 1):
        row = jnp.broadcast_to(ext_ref[d:d + 1, :] * LOG2E, (w, 2 * w))
        lo = pltpu.roll(row, w + 1, 1, stride=1, stride_axis=0)
        hi = pltpu.roll(row, 1, 1, stride=1, stride_axis=0)
        left.append(jnp.where((t >= first_col) & (t < first_col + WIN_COLS), lo, neg))
        right.append(jnp.where((t - w >= first_col) & (t - w < first_col + WIN_COLS), hi, neg))

    variants = [(0, 0), (rq, 0), (rows - rq, rows - nkr)]
    for v, (r0, ks) in enumerate(variants):
        for qi in range(rq):
            qr = r0 + qi
            rs = min(max(qr - WIN_ROWS // 2, 0), rows - WIN_ROWS)
            lead = rs - ks
            d0 = rs - qr + WIN_ROWS - 1

            def slab(kj, side):
                return side[d0 + kj - lead] if lead <= kj < lead + WIN_ROWS else neg

            for p in range(nkr // 2):
                tile = jnp.where(t < w, slab(2 * p, left), slab(2 * p + 1, right))
                o_ref[v, qi * w:(qi + 1) * w, p * 2 * w:(p + 1) * 2 * w] = tile


def _mod_kernel(c_ref, w_ref, b_ref, o_ref):
    c = c_ref[...]
    s = c * (1.0 / (1.0 + jnp.exp(-c)))
    o_ref[...] = jnp.dot(s, w_ref[...], precision=lax.Precision.HIGHEST,
                         preferred_element_type=F32) + b_ref[...]


def _modulation(cond, ada_w, ada_b):
    r, d = cond.shape
    depth, _, n = ada_w.shape
    tn = MOD_COL_TILE
    return pl.pallas_call(
        _mod_kernel,
        grid=(depth, n // tn),
        in_specs=[
            pl.BlockSpec((r, d), lambda l, j: (0, 0)),
            pl.BlockSpec((None, d, tn), lambda l, j: (l, 0, j)),
            pl.BlockSpec((None, 1, tn), lambda l, j: (l, 0, j)),
        ],
        out_specs=pl.BlockSpec((None, r, tn), lambda l, j: (l, 0, j)),
        out_shape=jax.ShapeDtypeStruct((depth, r, n), F32),
        compiler_params=_params("parallel", "parallel"),
        name="adaln_modulation",
    )(cond, ada_w, ada_b.reshape(depth, 1, n))


def _norm_modulate(x, g, shift, scale):
    ms = jnp.mean(x * x, axis=-1, keepdims=True)
    return (x * lax.rsqrt(ms + EPS)) * (g * (1.0 + scale)) + shift


def _twice_sigmoid(g_bf16):
    return jnp.tanh((g_bf16 * 0.5).astype(F32)) + 1.0


def _inproj_kernel(x_ref, mod_ref, g_ref, w_ref, *out_refs, segments, shift_row, scale_row):
    h = _norm_modulate(x_ref[...], g_ref[...],
                       mod_ref[0, shift_row:shift_row + 1, :], mod_ref[0, scale_row:scale_row + 1, :])
    hb = h.astype(BF16)
    for o_ref, (lo, hi) in zip(out_refs, segments):
        for c0 in range(lo, hi, PROJ_COL_CHUNK):
            c1 = min(c0 + PROJ_COL_CHUNK, hi)
            o_ref[:, c0 - lo:c1 - lo] = _dot(hb, w_ref[:, c0:c1]).astype(o_ref.dtype)


def _resident_weight_spec(w, layer):
    return pl.BlockSpec((None,) + w.shape[1:], lambda *_: (layer, 0, 0), pipeline_mode=pl.Buffered(1))


def _in_projection(x2d, mod, g, w_bf16, layer, segments, seg_dtypes, seq, tm, shift_row, scale_row):
    m, d = x2d.shape
    tiles_per_seq = seq // tm
    kern = functools.partial(_inproj_kernel, segments=tuple(segments),
                             shift_row=shift_row, scale_row=scale_row)
    return pl.pallas_call(
        kern,
        grid=(m // tm,),
        in_specs=[
            pl.BlockSpec((tm, d), lambda i: (i, 0)),
            pl.BlockSpec((1, N_MOD, d), lambda i: (i // tiles_per_seq, 0, 0)),
            pl.BlockSpec((1, d), lambda i: (0, 0)),
            _resident_weight_spec(w_bf16, layer),
        ],
        out_specs=[pl.BlockSpec((tm, hi - lo), lambda i: (i, 0)) for lo, hi in segments],
        out_shape=[jax.ShapeDtypeStruct((m, hi - lo), dt) for (lo, hi), dt in zip(segments, seg_dtypes)],
        compiler_params=_params("parallel"),
        name="norm_inproj",
    )(x2d, mod, g, w_bf16)


def _dft_stage1_kernel(x_ref, f_ref, o_ref):
    n1, rows, c = x_ref.shape
    t = DFT_ROWS_PER_GROUP
    f = f_ref[...].astype(BF16)
    for g in range(rows // t):
        x = x_ref[:, g * t:(g + 1) * t, :].reshape(n1 * t, c).astype(BF16)
        o_ref[:, :, g * t:(g + 1) * t, :] = _dot(f, x).reshape(2, n1, t, c)


def _dft_stage2_kernel(z_ref, g_ref, cs_ref, o_ref, *, n2):
    cos_t, sin_t = cs_ref[0].astype(BF16), cs_ref[1].astype(BF16)
    c = z_ref.shape[3]
    for j in range(z_ref.shape[1]):
        zz = z_ref[:, j].reshape(2 * n2, c).astype(BF16)
        p = _dot(g_ref[j].astype(BF16), zz).astype(BF16)
        o_ref[:, j, :] = _dot(p[:n2], cos_t) + _dot(p[n2:], sin_t)


def _fourier_latent(u, batch, seq):
    c = u.shape[1]
    n1, n2 = DFT_N1, seq // DFT_N1
    f1_np, g_np = _latent_dft_tables(seq)
    t = DFT_ROWS_PER_GROUP
    r1, r2 = t * DFT_STAGE1_GROUPS, t * DFT_STAGE2_GROUPS
    f1 = jnp.asarray(np.kron(f1_np, np.eye(t, dtype=np.float32)))
    g = jnp.asarray(g_np)
    cs = jnp.asarray(_channel_dft_tables())
    z = pl.pallas_call(
        _dft_stage1_kernel,
        grid=(batch, n2 // r1),
        in_specs=[
            pl.BlockSpec((None, n1, r1, c), lambda b, j: (b, 0, j, 0)),
            pl.BlockSpec((2 * n1 * t, n1 * t), lambda b, j: (0, 0)),
        ],
        out_specs=pl.BlockSpec((None, 2, n1, r1, c), lambda b, j: (b, 0, 0, j, 0)),
        out_shape=jax.ShapeDtypeStruct((batch, 2, n1, n2, c), F32),
        compiler_params=_params("parallel", "parallel"),
        name="dft_stage1",
    )(u.reshape(batch, n1, n2, c), f1)
    y = pl.pallas_call(
        functools.partial(_dft_stage2_kernel, n2=n2),
        grid=(n1 // r2, batch),
        in_specs=[
            pl.BlockSpec((None, 2, r2, n2, c), lambda k, b: (b, 0, k, 0, 0)),
            pl.BlockSpec((r2, 2 * n2, 2 * n2), lambda k, b: (k, 0, 0)),
            pl.BlockSpec((2, c, c), lambda k, b: (0, 0, 0)),
        ],
        out_specs=pl.BlockSpec((None, n2, r2, c), lambda k, b: (b, 0, k, 0)),
        out_shape=jax.ShapeDtypeStruct((batch, n2, n1, c), F32),
        compiler_params=_params("parallel", "parallel"),
        name="dft_stage2",
    )(z, g, cs)
    return y.reshape(batch * seq, c)


def _dft_context_kernel(u_ref, f_ref, cs_ref, o_ref, *, n):
    p = _dot(f_ref[...].astype(BF16), u_ref[...].astype(BF16)).astype(BF16)
    y = _dot(p[:n], cs_ref[0].astype(BF16)) + _dot(p[n:], cs_ref[1].astype(BF16))
    o_ref[...] = y.astype(o_ref.dtype)


def _fourier_context(u, batch, seq):
    c = u.shape[1]
    f = jnp.asarray(_context_dft_table(seq))
    cs = jnp.asarray(_channel_dft_tables())
    return pl.pallas_call(
        functools.partial(_dft_context_kernel, n=seq),
        grid=(batch,),
        in_specs=[
            pl.BlockSpec((seq, c), lambda b: (b, 0)),
            pl.BlockSpec((2 * seq, seq), lambda b: (0, 0)),
            pl.BlockSpec((2, c, c), lambda b: (0, 0, 0)),
        ],
        out_specs=pl.BlockSpec((seq, c), lambda b: (b, 0)),
        out_shape=jax.ShapeDtypeStruct((batch * seq, c), BF16),
        compiler_params=_params("parallel"),
        name="dft_context",
    )(u, f, cs)


def _attend_heads(problems):
    q_scale = HEAD_DIM ** -0.5 * LOG2E

    def lanes(h):
        return slice((h // 2) * LANES, (h // 2 + 1) * LANES)

    def own_lanes(shape, h):
        low = lax.broadcasted_iota(jnp.int32, shape, 1) < HEAD_DIM
        return low if h % 2 == 0 else jnp.logical_not(low)

    def logits(unit):
        h, g = unit
        q_ref, key_refs, _, bias_fn, _ = problems[g]
        q_pair = q_ref[:, lanes(h)]
        qs = (q_pair.astype(F32) * q_scale).astype(BF16)
        qm = jnp.where(own_lanes(qs.shape, h), qs, jnp.zeros_like(qs))
        out = []
        for j, k_ref in enumerate(key_refs):
            sj = _dot_nt(qm, k_ref[:, lanes(h)])
            bj = bias_fn(h, j)
            out.append(sj if bj is None else sj + bj)
        return out

    units = [(h, g) for h in range(N_NA_HEADS) for g in range(len(problems))]
    ahead = ATTN_LOGITS_AHEAD
    queue = [logits(u) for u in units[:ahead]]
    even_out = [None] * len(problems)
    for n, (h, g) in enumerate(units):
        s = queue.pop(0)
        if n + ahead < len(units):
            queue.append(logits(units[n + ahead]))
        _, _, value_refs, _, o_ref = problems[g]
        m = s[0]
        for sj in s[1:]:
            m = jnp.maximum(m, sj)
        m = jnp.max(m, axis=-1, keepdims=True)
        p = jnp.concatenate([jnp.exp2(sj - m).astype(BF16) for sj in s], axis=1)
        v = jnp.concatenate([v_ref[:, lanes(h)] for v_ref in value_refs], axis=0)
        v_aug = jnp.where(own_lanes(v.shape, h), v, jnp.ones_like(v))
        acc = _dot(p, v_aug)
        denom = pltpu.roll(acc, HEAD_DIM, 1)
        out = acc * (1.0 / denom)
        if h % 2 == 0:
            even_out[g] = out
        else:
            o_ref[:, lanes(h)] = jnp.where(own_lanes(out.shape, 0), even_out[g], out).astype(o_ref.dtype)


def _natten_kernel(q_ref, *refs):
    nb, ng, tq = ATTN_KEY_BLOCKS, ATTN_BLOCKS_PER_STEP, ATTN_Q
    kv_refs, (kc_ref, vc_ref), bias_refs, o_ref = (
        refs[:2 * nb * ng], refs[2 * nb * ng:2 * nb * ng + 2], refs[2 * nb * ng + 2:-1], refs[-1])

    def problem(g):
        k_refs = kv_refs[2 * nb * g:2 * nb * g + nb]
        v_refs = kv_refs[2 * nb * g + nb:2 * nb * (g + 1)]
        bias_ref = bias_refs[g]

        def bias_fn(h, j):
            return bias_ref[h, :, j * tq:(j + 1) * tq] if j < nb else None

        rows = pl.ds(g * tq, tq)
        return (q_ref.at[rows], list(k_refs) + [kc_ref], list(v_refs) + [vc_ref], bias_fn, o_ref.at[rows])

    _attend_heads([problem(g) for g in range(ng)])


def _neighbourhood_attention(qkv, kv_ctx, ctx_k_col, bias_tables, layer, batch, seq, ctx_len):
    m = qkv.shape[0]
    tq, ng = ATTN_Q, ATTN_BLOCKS_PER_STEP
    nrb = seq // tq
    assert ctx_len == tq, "context keys are processed as one key block"

    def q_map(b, r):
        return (b * (nrb // ng) + r, 0)

    def kv_map(g, j, col):
        return lambda b, r: (b * nrb + _attn_key_block_start(ng * r + g, nrb) + j, col)

    def bias_map(g):
        def index(b, r):
            rb = ng * r + g
            return (layer, jnp.where(rb == 0, 0, jnp.where(rb == nrb - 1, 2, 1)), 0, 0, 0)
        return index

    blk = (tq, NA_WIDTH)
    in_specs = [pl.BlockSpec((ng * tq, NA_WIDTH), q_map)]
    for g in range(ng):
        in_specs += [pl.BlockSpec(blk, kv_map(g, j, 1)) for j in range(ATTN_KEY_BLOCKS)]
        in_specs += [pl.BlockSpec(blk, kv_map(g, j, 2)) for j in range(ATTN_KEY_BLOCKS)]
    in_specs += [
        pl.BlockSpec((ctx_len, NA_WIDTH), lambda b, r: (b, ctx_k_col)),
        pl.BlockSpec((ctx_len, NA_WIDTH), lambda b, r: (b, ctx_k_col + 1)),
    ]
    in_specs += [pl.BlockSpec((None, None, N_NA_HEADS, tq, ATTN_KEY_BLOCKS * tq), bias_map(g))
                 for g in range(ng)]
    args = [qkv] * (1 + 2 * ATTN_KEY_BLOCKS * ng) + [kv_ctx, kv_ctx] + [bias_tables] * ng
    return pl.pallas_call(
        _natten_kernel,
        grid=(batch, nrb // ng),
        in_specs=in_specs,
        out_specs=pl.BlockSpec((ng * tq, NA_WIDTH), q_map),
        out_shape=jax.ShapeDtypeStruct((m, NA_WIDTH), BF16),
        compiler_params=_params("parallel", "arbitrary"),
        name="neighbourhood_attention",
    )(*args)


def _ctx_attn_kernel(q_ref, k_ref, v_ref, o_ref):
    _attend_heads([(q_ref, [k_ref], [v_ref], lambda h, j: None, o_ref)])


def _context_attention(qkv, batch, ctx_len):
    blk = (ctx_len, NA_WIDTH)
    return pl.pallas_call(
        _ctx_attn_kernel,
        grid=(batch,),
        in_specs=[pl.BlockSpec(blk, lambda b: (b, 0)),
                  pl.BlockSpec(blk, lambda b: (b, 1)),
                  pl.BlockSpec(blk, lambda b: (b, 2))],
        out_specs=pl.BlockSpec(blk, lambda b: (b, 0)),
        out_shape=jax.ShapeDtypeStruct((batch * ctx_len, NA_WIDTH), BF16),
        compiler_params=_params("parallel"),
        name="context_attention",
    )(qkv, qkv, qkv)


HALO_ROWS = 16


def _merge_mlp_kernel(x_ref, mod_ref, f_ref, cv_ref, cvp_ref, cvn_ref, at_ref, gt_ref, cw_ref,
                      wf_ref, wc_ref, wa_ref, wo_ref, g2_ref, w1_ref, w2_ref, fg_ref, o_ref,
                      cv_scr, x1_scr, *, tiles_per_seq, final_norm):
    i = pl.program_id(0)
    tm = x_ref.shape[0]
    cw = CONV_WIDTH
    has_prev = (i % tiles_per_seq) != 0
    has_next = (i % tiles_per_seq) != tiles_per_seq - 1

    u = cv_ref[:, 0:cw].astype(F32)
    gb = cv_ref[:, cw:2 * cw].astype(F32)
    gc = cv_ref[:, 2 * cw:3 * cw].astype(F32)
    z = gc * u
    zp_row = (cvp_ref[HALO_ROWS - 1:HALO_ROWS, 2 * cw:3 * cw].astype(F32)
              * cvp_ref[HALO_ROWS - 1:HALO_ROWS, 0:cw].astype(F32))
    zn_row = cvn_ref[0:1, 2 * cw:3 * cw].astype(F32) * cvn_ref[0:1, 0:cw].astype(F32)
    zp_row = jnp.where(has_prev, zp_row, 0.0)
    zn_row = jnp.where(has_next, zn_row, 0.0)
    row = lax.broadcasted_iota(jnp.int32, (tm, cw), 0)
    z_prev = jnp.where(row == 0, zp_row, pltpu.roll(z, 1, 0))
    z_next = jnp.where(row == tm - 1, zn_row, pltpu.roll(z, tm - 1, 0))
    y = cw_ref[0:1, :] * z_prev + cw_ref[1:2, :] * z + cw_ref[2:3, :] * z_next
    cv_scr[...] = (gb * y).astype(BF16)

    d = D_MODEL
    gate = mod_ref[0, 2:3, :]

    def branches(c):
        r = slice(c * MERGE_ROW_CHUNK, (c + 1) * MERGE_ROW_CHUNK)
        return (_dot(f_ref[r, :].astype(BF16), wf_ref[...]), _dot(cv_scr[r, :], wc_ref[...]),
                _dot(at_ref[r, :], wa_ref[...]))

    nxt = branches(0)
    n_chunks = tm // MERGE_ROW_CHUNK
    for c in range(n_chunks):
        r = slice(c * MERGE_ROW_CHUNK, (c + 1) * MERGE_ROW_CHUNK)
        pf, pc, pa = nxt
        if c + 1 < n_chunks:
            nxt = branches(c + 1)
        mix = _twice_sigmoid(gt_ref[r, 0:d]) * pf
        mix = mix + _twice_sigmoid(gt_ref[r, d:2 * d]) * pc
        mix = mix + _twice_sigmoid(gt_ref[r, 2 * d:3 * d]) * pa
        proj = _dot(mix.astype(BF16), wo_ref[...])
        x1_scr[r, :] = x_ref[r, :] + (0.5 * gate) * proj

    x1 = x1_scr[...]
    h = _norm_modulate(x1, g2_ref[...], mod_ref[0, 3:4, :], mod_ref[0, 4:5, :]).astype(BF16)
    acc = None
    for c0 in range(0, MLP_HIDDEN, MLP_HIDDEN_CHUNK):
        a = jnp.maximum(_dot(h, w1_ref[:, c0:c0 + MLP_HIDDEN_CHUNK]), 0.0)
        part = _dot((a * a).astype(BF16), w2_ref[c0:c0 + MLP_HIDDEN_CHUNK, :])
        acc = part if acc is None else acc + part
    y = x1 + mod_ref[0, 5:6, :] * acc
    if final_norm:
        ms = jnp.mean(y * y, axis=-1, keepdims=True)
        y = y * lax.rsqrt(ms + EPS) * fg_ref[...]
    o_ref[...] = y


def _merge_mlp(x2d, mod, f, conv, attn, gates, conv_w, w_f, w_c, w_a, w_o, g2, w1, w2, final_g,
               layer, seq, tm, final_norm):
    m, d = x2d.shape
    tiles_per_seq = seq // tm
    halo_per_tile = tm // HALO_ROWS
    n_halo = m // HALO_ROWS
    kern = functools.partial(_merge_mlp_kernel, tiles_per_seq=tiles_per_seq, final_norm=final_norm)
    const = lambda i: (0, 0)
    return pl.pallas_call(
        kern,
        grid=(m // tm,),
        in_specs=[
            pl.BlockSpec((tm, d), lambda i: (i, 0)),
            pl.BlockSpec((1, N_MOD, d), lambda i: (i // tiles_per_seq, 0, 0)),
            pl.BlockSpec((tm, FOURIER_WIDTH), lambda i: (i, 0)),
            pl.BlockSpec((tm, 3 * CONV_WIDTH), lambda i: (i, 0)),
            pl.BlockSpec((HALO_ROWS, 3 * CONV_WIDTH),
                         lambda i: (jnp.maximum(i * halo_per_tile - 1, 0), 0)),
            pl.BlockSpec((HALO_ROWS, 3 * CONV_WIDTH),
                         lambda i: (jnp.minimum((i + 1) * halo_per_tile, n_halo - 1), 0)),
            pl.BlockSpec((tm, NA_WIDTH), lambda i: (i, 0)),
            pl.BlockSpec((tm, 3 * d), lambda i: (i, 0)),
            pl.BlockSpec((CONV_K, CONV_WIDTH), const),
            _resident_weight_spec(w_f, layer),
            _resident_weight_spec(w_c, layer),
            _resident_weight_spec(w_a, layer),
            _resident_weight_spec(w_o, layer),
            pl.BlockSpec((1, d), const),
            _resident_weight_spec(w1, layer),
            _resident_weight_spec(w2, layer),
            pl.BlockSpec((1, d), const),
        ],
        out_specs=pl.BlockSpec((tm, d), lambda i: (i, 0)),
        out_shape=jax.ShapeDtypeStruct((m, d), F32),
        scratch_shapes=[pltpu.VMEM((tm, CONV_WIDTH), BF16), pltpu.VMEM((tm, d), F32)],
        compiler_params=_params("parallel"),
        name="merge_mlp",
    )(x2d, mod, f, conv, conv, conv, attn, gates, conv_w, w_f, w_c, w_a, w_o, g2, w1, w2, final_g)


ALL_SEGMENTS = (SEG_FOURIER, SEG_CONV, SEG_QKV, SEG_GATES)
SEGMENT_DTYPES = (F32, BF16, BF16, BF16)


def kernel(x, c, ctx, c_ctx, ada_w, ada_b, norm1_g, norm2_g, w_in, conv_w, rel_bias,
           w_fourier, w_conv, w_attn, w_o, mlp_w1, mlp_w2, final_g):
    batch, seq, d = x.shape
    ctx_len = ctx.shape[1]
    rows = seq // GRID_W
    depth = ada_w.shape[0]

    n_cond = batch + 1
    pad = (-n_cond) % 8
    cond = jnp.concatenate([c, c_ctx[None, :], jnp.zeros((pad, d), F32)], axis=0)
    mod_all = _modulation(cond, ada_w, ada_b)

    x2 = x.reshape(batch * seq, d)
    c2 = ctx.reshape(batch * ctx_len, d)
    fg = final_g.reshape(1, d)
    bias_tables = _attn_bias_table(rel_bias, rows)
    w_in_b, w_f_b, w_c_b, w_a_b, w_o_b, w1_b, w2_b = (
        w.astype(BF16) for w in (w_in, w_fourier, w_conv, w_attn, w_o, mlp_w1, mlp_w2))

    for l in range(depth):
        last = l == depth - 1
        mod_x = mod_all[l, :batch].reshape(batch, N_MOD, d)
        mod_c = jnp.broadcast_to(mod_all[l, batch].reshape(1, N_MOD, d), (batch, N_MOD, d))
        g1 = norm1_g[l].reshape(1, d)
        g2 = norm2_g[l].reshape(1, d)

        if last:
            (kv_c,) = _in_projection(c2, mod_c, g1, w_in_b[l:l + 1, :, KV_START:KV_END], 0,
                                     ((0, KV_END - KV_START),), (BF16,), ctx_len, ctx_len, 0, 1)
            ctx_k_col = 0
        else:
            uf_c, conv_c, qkv_c, gates_c = _in_projection(
                c2, mod_c, g1, w_in_b, l, ALL_SEGMENTS, SEGMENT_DTYPES, ctx_len, ctx_len, 0, 1)
            kv_c, ctx_k_col = qkv_c, 1

        uf_x, conv_x, qkv_x, gates_x = _in_projection(
            x2, mod_x, g1, w_in_b, l, ALL_SEGMENTS, SEGMENT_DTYPES, seq, TM_INPROJ, 0, 1)
        f_x = _fourier_latent(uf_x, batch, seq)
        at_x = _neighbourhood_attention(qkv_x, kv_c, ctx_k_col, bias_tables, l, batch, seq, ctx_len)
        x2 = _merge_mlp(x2, mod_x, f_x, conv_x, at_x, gates_x, conv_w[l], w_f_b, w_c_b, w_a_b, w_o_b,
                        g2, w1_b, w2_b, fg, l, seq, TM_LATENT, final_norm=last)
        if not last:
            f_c = _fourier_context(uf_c, batch, ctx_len)
            at_c = _context_attention(qkv_c, batch, ctx_len)
            c2 = _merge_mlp(c2, mod_c, f_c, conv_c, at_c, gates_c, conv_w[l], w_f_b, w_c_b, w_a_b, w_o_b,
                            g2, w1_b, w2_b, fg, l, ctx_len, ctx_len, final_norm=False)

    return x2.reshape(batch, seq, d)
```

```python
import functools
import math

import numpy as np
import jax
import jax.numpy as jnp
from jax import lax
from jax.experimental import pallas as pl
from jax.experimental.pallas import tpu as pltpu

D_MODEL = 1024
DEPTH = 2
GRID_W = 64
HEAD_DIM = 64
N_NA_HEADS = 8
NA_WIDTH = N_NA_HEADS * HEAD_DIM
N_FOURIER_GROUPS = 4
FOURIER_WIDTH = D_MODEL // 4
FOURIER_GROUP = FOURIER_WIDTH // N_FOURIER_GROUPS
CONV_WIDTH = D_MODEL // 4
CONV_K = 3
WIN_ROWS = 8
WIN_COLS = 16
MLP_HIDDEN = 4 * D_MODEL
N_MOD = 6
EPS = 1e-6
NEG = -1e30
LOG2E = math.log2(math.e)

SEG_FOURIER = (0, FOURIER_WIDTH)
SEG_CONV = (SEG_FOURIER[1], SEG_FOURIER[1] + 3 * CONV_WIDTH)
SEG_QKV = (SEG_CONV[1], SEG_CONV[1] + 3 * NA_WIDTH)
SEG_GATES = (SEG_QKV[1], SEG_QKV[1] + 3 * D_MODEL)
KV_START = SEG_QKV[0] + NA_WIDTH
KV_END = SEG_QKV[1]

LANES = 128
VMEM_LIMIT_BYTES = 56 * 1024 * 1024

TM_LATENT = 512
TM_INPROJ = 1024
ROWS_PER_ATTN_STEP = 4
ATTN_Q = ROWS_PER_ATTN_STEP * GRID_W
ATTN_KEY_BLOCKS = 3
ATTN_LOGITS_AHEAD = 1
ATTN_BLOCKS_PER_STEP = 2
DFT_N1 = 64
DFT_ROWS_PER_GROUP = 8
DFT_STAGE1_GROUPS = 8
DFT_STAGE2_GROUPS = 4
MOD_COL_TILE = 1536
PROJ_COL_CHUNK = 512
MLP_HIDDEN_CHUNK = 1024
MERGE_ROW_CHUNK = 256

BF16 = jnp.bfloat16
F32 = jnp.float32


def _params(*semantics):
    return pltpu.CompilerParams(dimension_semantics=semantics, vmem_limit_bytes=VMEM_LIMIT_BYTES)


def _dot(a, b):
    return jnp.dot(a, b, preferred_element_type=F32)


def _dot_nt(a, b):
    return lax.dot_general(a, b, (((1,), (1,)), ((), ())), preferred_element_type=F32)


def _channel_dft_tables():
    c = np.arange(FOURIER_GROUP)
    ang = 2.0 * np.pi * ((c[:, None] * c[None, :]) % FOURIER_GROUP) / FOURIER_GROUP
    eye = np.eye(N_FOURIER_GROUPS)
    s = 1.0 / math.sqrt(FOURIER_GROUP)
    return np.stack([np.kron(eye, np.cos(ang) * s), np.kron(eye, np.sin(ang) * s)]).astype(np.float32)


def _latent_dft_tables(n):
    n1, n2 = DFT_N1, n // DFT_N1
    a = np.arange(n1)
    ang1 = 2.0 * np.pi * ((a[:, None] * a[None, :]) % n1) / n1
    s1 = 1.0 / math.sqrt(n1)
    f1 = np.concatenate([np.cos(ang1) * s1, -np.sin(ang1) * s1], axis=0)
    k1 = np.arange(n1)[:, None, None]
    k2 = np.arange(n2)[None, :, None]
    m2 = np.arange(n2)[None, None, :]
    ang = 2.0 * np.pi * ((m2 * (k1 + n1 * k2)) % n) / n
    s2 = 1.0 / math.sqrt(n2)
    tr, ti = np.cos(ang) * s2, -np.sin(ang) * s2
    g = np.concatenate([np.concatenate([tr, -ti], axis=2),
                        np.concatenate([ti, tr], axis=2)], axis=1)
    return f1.astype(np.float32), g.astype(np.float32)


def _context_dft_table(n):
    a = np.arange(n)
    ang = 2.0 * np.pi * ((a[:, None] * a[None, :]) % n) / n
    s = 1.0 / math.sqrt(n)
    return np.concatenate([np.cos(ang) * s, -np.sin(ang) * s], axis=0).astype(np.float32)


def _attn_key_block_start(rb, n_row_blocks):
    lead_blocks = (WIN_ROWS // 2) // ROWS_PER_ATTN_STEP
    return jnp.clip(rb - lead_blocks, 0, n_row_blocks - ATTN_KEY_BLOCKS)


def _attn_bias_table(rel_bias, rows):
    depth, h, ndr, _ = rel_bias.shape
    w = GRID_W
    rq, nkr = ROWS_PER_ATTN_STEP, ROWS_PER_ATTN_STEP * ATTN_KEY_BLOCKS
    ends = w - WIN_COLS
    ext = jnp.concatenate([
        jnp.broadcast_to(rel_bias[..., :1], (depth, h, ndr, ends)), rel_bias,
        jnp.broadcast_to(rel_bias[..., -1:], (depth, h, ndr, ends)), jnp.zeros((depth, h, ndr, 1), F32)], axis=-1)
    return pl.pallas_call(
        functools.partial(_bias_table_kernel, rows=rows),
        grid=(depth, h),
        in_specs=[pl.BlockSpec((None, None, ndr, 2 * w), lambda l, hh: (l, hh, 0, 0))],
        out_specs=pl.BlockSpec((None, 3, None, rq * w, nkr * w), lambda l, hh: (l, 0, hh, 0, 0)),
        out_shape=jax.ShapeDtypeStruct((depth, 3, h, rq * w, nkr * w), F32),
        compiler_params=_params("parallel", "parallel"),
        name="attn_bias_table",
    )(ext)


def _bias_table_kernel(ext_ref, o_ref, *, rows):
    w = GRID_W
    rq, nkr = ROWS_PER_ATTN_STEP, ROWS_PER_ATTN_STEP * ATTN_KEY_BLOCKS
    q = lax.broadcasted_iota(jnp.int32, (w, 2 * w), 0)
    t = lax.broadcasted_iota(jnp.int32, (w, 2 * w), 1)
    first_col = jnp.clip(q - WIN_COLS // 2, 0, w - WIN_COLS)
    neg = jnp.full((w, 2 * w), NEG, F32)
    left, right = [], []
    for d in range(2 * WIN_ROWS - 1):
        row = jnp.broadcast_to(ext_ref[d:d + 1, :] * LOG2E, (w, 2 * w))
        lo = pltpu.roll(row, w + 1, 1, stride=1, stride_axis=0)
        hi = pltpu.roll(row, 1, 1, stride=1, stride_axis=0)
        left.append(jnp.where((t >= first_col) & (t < first_col + WIN_COLS), lo, neg))
        right.append(jnp.where((t - w >= first_col) & (t - w < first_col + WIN_COLS), hi, neg))

    variants = [(0, 0), (rq, 0), (rows - rq, rows - nkr)]
    for v, (r0, ks) in enumerate(variants):
        for qi in range(rq):
            qr = r0 + qi
            rs = min(max(qr - WIN_ROWS // 2, 0), rows - WIN_ROWS)
            lead = rs - ks
            d0 = rs - qr + WIN_ROWS - 1

            def slab(kj, side):
                return side[d0 + kj - lead] if lead <= kj < lead + WIN_ROWS else neg

            for p in range(nkr // 2):
                tile = jnp.where(t < w, slab(2 * p, left), slab(2 * p + 1, right))
                o_ref[v, qi * w:(qi + 1) * w, p * 2 * w:(p + 1) * 2 * w] = tile


def _mod_kernel(c_ref, w_ref, b_ref, o_ref):
    c = c_ref[...]
    s = c * (1.0 / (1.0 + jnp.exp(-c)))
    o_ref[...] = jnp.dot(s, w_ref[...], precision=lax.Precision.HIGHEST,
                         preferred_element_type=F32) + b_ref[...]


def _modulation(cond, ada_w, ada_b):
    r, d = cond.shape
    depth, _, n = ada_w.shape
    tn = MOD_COL_TILE
    return pl.pallas_call(
        _mod_kernel,
        grid=(depth, n // tn),
        in_specs=[
            pl.BlockSpec((r, d), lambda l, j: (0, 0)),
            pl.BlockSpec((None, d, tn), lambda l, j: (l, 0, j)),
            pl.BlockSpec((None, 1, tn), lambda l, j: (l, 0, j)),
        ],
        out_specs=pl.BlockSpec((None, r, tn), lambda l, j: (l, 0, j)),
        out_shape=jax.ShapeDtypeStruct((depth, r, n), F32),
        compiler_params=_params("parallel", "parallel"),
        name="adaln_modulation",
    )(cond, ada_w, ada_b.reshape(depth, 1, n))


def _norm_modulate(x, g, shift, scale):
    ms = jnp.mean(x * x, axis=-1, keepdims=True)
    return (x * lax.rsqrt(ms + EPS)) * (g * (1.0 + scale)) + shift


def _twice_sigmoid(g_bf16):
    return jnp.tanh((g_bf16 * 0.5).astype(F32)) + 1.0


def _inproj_kernel(x_ref, mod_ref, g_ref, w_ref, *out_refs, segments, shift_row, scale_row):
    h = _norm_modulate(x_ref[...], g_ref[...],
                       mod_ref[0, shift_row:shift_row + 1, :], mod_ref[0, scale_row:scale_row + 1, :])
    hb = h.astype(BF16)
    for o_ref, (lo, hi) in zip(out_refs, segments):
        for c0 in range(lo, hi, PROJ_COL_CHUNK):
            c1 = min(c0 + PROJ_COL_CHUNK, hi)
            o_ref[:, c0 - lo:c1 - lo] = _dot(hb, w_ref[:, c0:c1]).astype(o_ref.dtype)


def _resident_weight_spec(w, layer):
    return pl.BlockSpec((None,) + w.shape[1:], lambda *_: (layer, 0, 0), pipeline_mode=pl.Buffered(1))


def _tiles_per_mod_row(m, mod, tm):
    rows_per_group = m // mod.shape[0]
    assert m % mod.shape[0] == 0 and rows_per_group % tm == 0
    return rows_per_group // tm


def _in_projection(x2d, mod, g, w_bf16, layer, segments, seg_dtypes, tm, shift_row, scale_row):
    m, d = x2d.shape
    tiles_per_seq = _tiles_per_mod_row(m, mod, tm)
    kern = functools.partial(_inproj_kernel, segments=tuple(segments),
                             shift_row=shift_row, scale_row=scale_row)
    return pl.pallas_call(
        kern,
        grid=(m // tm,),
        in_specs=[
            pl.BlockSpec((tm, d), lambda i: (i, 0)),
            pl.BlockSpec((1, N_MOD, d), lambda i: (i // tiles_per_seq, 0, 0)),
            pl.BlockSpec((1, d), lambda i: (0, 0)),
            _resident_weight_spec(w_bf16, layer),
        ],
        out_specs=[pl.BlockSpec((tm, hi - lo), lambda i: (i, 0)) for lo, hi in segments],
        out_shape=[jax.ShapeDtypeStruct((m, hi - lo), dt) for (lo, hi), dt in zip(segments, seg_dtypes)],
        compiler_params=_params("parallel"),
        name="norm_inproj",
    )(x2d, mod, g, w_bf16)


def _dft_stage1_kernel(x_ref, f_ref, o_ref):
    n1, rows, c = x_ref.shape
    t = DFT_ROWS_PER_GROUP
    f = f_ref[...].astype(BF16)
    for g in range(rows // t):
        x = x_ref[:, g * t:(g + 1) * t, :].reshape(n1 * t, c).astype(BF16)
        o_ref[:, :, g * t:(g + 1) * t, :] = _dot(f, x).reshape(2, n1, t, c)


def _dft_stage2_kernel(z_ref, g_ref, cs_ref, o_ref, *, n2):
    cos_t, sin_t = cs_ref[0].astype(BF16), cs_ref[1].astype(BF16)
    c = z_ref.shape[3]
    for j in range(z_ref.shape[1]):
        zz = z_ref[:, j].reshape(2 * n2, c).astype(BF16)
        p = _dot(g_ref[j].astype(BF16), zz).astype(BF16)
        o_ref[:, j, :] = _dot(p[:n2], cos_t) + _dot(p[n2:], sin_t)


def _fourier_latent(u, batch, seq):
    c = u.shape[1]
    n1, n2 = DFT_N1, seq // DFT_N1
    f1_np, g_np = _latent_dft_tables(seq)
    t = DFT_ROWS_PER_GROUP
    r1, r2 = t * DFT_STAGE1_GROUPS, t * DFT_STAGE2_GROUPS
    f1 = jnp.asarray(np.kron(f1_np, np.eye(t, dtype=np.float32)))
    g = jnp.asarray(g_np)
    cs = jnp.asarray(_channel_dft_tables())
    z = pl.pallas_call(
        _dft_stage1_kernel,
        grid=(batch, n2 // r1),
        in_specs=[
            pl.BlockSpec((None, n1, r1, c), lambda b, j: (b, 0, j, 0)),
            pl.BlockSpec((2 * n1 * t, n1 * t), lambda b, j: (0, 0)),
        ],
        out_specs=pl.BlockSpec((None, 2, n1, r1, c), lambda b, j: (b, 0, 0, j, 0)),
        out_shape=jax.ShapeDtypeStruct((batch, 2, n1, n2, c), F32),
        compiler_params=_params("parallel", "parallel"),
        name="dft_stage1",
    )(u.reshape(batch, n1, n2, c), f1)
    y = pl.pallas_call(
        functools.partial(_dft_stage2_kernel, n2=n2),
        grid=(n1 // r2, batch),
        in_specs=[
            pl.BlockSpec((None, 2, r2, n2, c), lambda k, b: (b, 0, k, 0, 0)),
            pl.BlockSpec((r2, 2 * n2, 2 * n2), lambda k, b: (k, 0, 0)),
            pl.BlockSpec((2, c, c), lambda k, b: (0, 0, 0)),
        ],
        out_specs=pl.BlockSpec((None, n2, r2, c), lambda k, b: (b, 0, k, 0)),
        out_shape=jax.ShapeDtypeStruct((batch, n2, n1, c), F32),
        compiler_params=_params("parallel", "parallel"),
        name="dft_stage2",
    )(z, g, cs)
    return y.reshape(batch * seq, c)


def _dft_context_kernel(u_ref, f_ref, cs_ref, o_ref, *, n):
    p = _dot(f_ref[...].astype(BF16), u_ref[...].astype(BF16)).astype(BF16)
    y = _dot(p[:n], cs_ref[0].astype(BF16)) + _dot(p[n:], cs_ref[1].astype(BF16))
    o_ref[...] = y.astype(o_ref.dtype)


def _fourier_context(u, batch, seq):
    c = u.shape[1]
    f = jnp.asarray(_context_dft_table(seq))
    cs = jnp.asarray(_channel_dft_tables())
    return pl.pallas_call(
        functools.partial(_dft_context_kernel, n=seq),
        grid=(batch,),
        in_specs=[
            pl.BlockSpec((seq, c), lambda b: (b, 0)),
            pl.BlockSpec((2 * seq, seq), lambda b: (0, 0)),
            pl.BlockSpec((2, c, c), lambda b: (0, 0, 0)),
        ],
        out_specs=pl.BlockSpec((seq, c), lambda b: (b, 0)),
        out_shape=jax.ShapeDtypeStruct((batch * seq, c), BF16),
        compiler_params=_params("parallel"),
        name="dft_context",
    )(u, f, cs)


def _attend_heads(problems):
    q_scale = HEAD_DIM ** -0.5 * LOG2E

    def lanes(h):
        return slice((h // 2) * LANES, (h // 2 + 1) * LANES)

    def own_lanes(shape, h):
        low = lax.broadcasted_iota(jnp.int32, shape, 1) < HEAD_DIM
        return low if h % 2 == 0 else jnp.logical_not(low)

    def logits(unit):
        h, g = unit
        q_ref, key_refs, _, bias_fn, _ = problems[g]
        q_pair = q_ref[:, lanes(h)]
        qs = (q_pair.astype(F32) * q_scale).astype(BF16)
        qm = jnp.where(own_lanes(qs.shape, h), qs, jnp.zeros_like(qs))
        out = []
        for j, k_ref in enumerate(key_refs):
            sj = _dot_nt(qm, k_ref[:, lanes(h)])
            bj = bias_fn(h, j)
            out.append(sj if bj is None else sj + bj)
        return out

    units = [(h, g) for h in range(N_NA_HEADS) for g in range(len(problems))]
    ahead = ATTN_LOGITS_AHEAD
    queue = [logits(u) for u in units[:ahead]]
    even_out = [None] * len(problems)
    for n, (h, g) in enumerate(units):
        s = queue.pop(0)
        if n + ahead < len(units):
            queue.append(logits(units[n + ahead]))
        _, _, value_refs, _, o_ref = problems[g]
        m = s[0]
        for sj in s[1:]:
            m = jnp.maximum(m, sj)
        m = jnp.max(m, axis=-1, keepdims=True)
        p = jnp.concatenate([jnp.exp2(sj - m).astype(BF16) for sj in s], axis=1)
        v = jnp.concatenate([v_ref[:, lanes(h)] for v_ref in value_refs], axis=0)
        v_aug = jnp.where(own_lanes(v.shape, h), v, jnp.ones_like(v))
        acc = _dot(p, v_aug)
        denom = pltpu.roll(acc, HEAD_DIM, 1)
        out = acc * (1.0 / denom)
        if h % 2 == 0:
            even_out[g] = out
        else:
            o_ref[:, lanes(h)] = jnp.where(own_lanes(out.shape, 0), even_out[g], out).astype(o_ref.dtype)


def _natten_kernel(q_ref, *refs):
    nb, ng, tq = ATTN_KEY_BLOCKS, ATTN_BLOCKS_PER_STEP, ATTN_Q
    kv_refs, (kc_ref, vc_ref), bias_refs, o_ref = (
        refs[:2 * nb * ng], refs[2 * nb * ng:2 * nb * ng + 2], refs[2 * nb * ng + 2:-1], refs[-1])

    def problem(g):
        k_refs = kv_refs[2 * nb * g:2 * nb * g + nb]
        v_refs = kv_refs[2 * nb * g + nb:2 * nb * (g + 1)]
        bias_ref = bias_refs[g]

        def bias_fn(h, j):
            return bias_ref[h, :, j * tq:(j + 1) * tq] if j < nb else None

        rows = pl.ds(g * tq, tq)
        return (q_ref.at[rows], list(k_refs) + [kc_ref], list(v_refs) + [vc_ref], bias_fn, o_ref.at[rows])

    _attend_heads([problem(g) for g in range(ng)])


def _neighbourhood_attention(qkv, kv_ctx, ctx_k_col, bias_tables, layer, batch, seq, ctx_len):
    m = qkv.shape[0]
    tq, ng = ATTN_Q, ATTN_BLOCKS_PER_STEP
    nrb = seq // tq
    assert ctx_len == tq, "context keys are processed as one key block"

    def q_map(b, r):
        return (b * (nrb // ng) + r, 0)

    def kv_map(g, j, col):
        return lambda b, r: (b * nrb + _attn_key_block_start(ng * r + g, nrb) + j, col)

    def bias_map(g):
        def index(b, r):
            rb = ng * r + g
            return (layer, jnp.where(rb == 0, 0, jnp.where(rb == nrb - 1, 2, 1)), 0, 0, 0)
        return index

    blk = (tq, NA_WIDTH)
    in_specs = [pl.BlockSpec((ng * tq, NA_WIDTH), q_map)]
    for g in range(ng):
        in_specs += [pl.BlockSpec(blk, kv_map(g, j, 1)) for j in range(ATTN_KEY_BLOCKS)]
        in_specs += [pl.BlockSpec(blk, kv_map(g, j, 2)) for j in range(ATTN_KEY_BLOCKS)]
    in_specs += [
        pl.BlockSpec((ctx_len, NA_WIDTH), lambda b, r: (b, ctx_k_col)),
        pl.BlockSpec((ctx_len, NA_WIDTH), lambda b, r: (b, ctx_k_col + 1)),
    ]
    in_specs += [pl.BlockSpec((None, None, N_NA_HEADS, tq, ATTN_KEY_BLOCKS * tq), bias_map(g))
                 for g in range(ng)]
    args = [qkv] * (1 + 2 * ATTN_KEY_BLOCKS * ng) + [kv_ctx, kv_ctx] + [bias_tables] * ng
    return pl.pallas_call(
        _natten_kernel,
        grid=(batch, nrb // ng),
        in_specs=in_specs,
        out_specs=pl.BlockSpec((ng * tq, NA_WIDTH), q_map),
        out_shape=jax.ShapeDtypeStruct((m, NA_WIDTH), BF16),
        compiler_params=_params("parallel", "arbitrary"),
        name="neighbourhood_attention",
    )(*args)


def _ctx_attn_kernel(q_ref, k_ref, v_ref, o_ref):
    _attend_heads([(q_ref, [k_ref], [v_ref], lambda h, j: None, o_ref)])


def _context_attention(qkv, batch, ctx_len):
    blk = (ctx_len, NA_WIDTH)
    return pl.pallas_call(
        _ctx_attn_kernel,
        grid=(batch,),
        in_specs=[pl.BlockSpec(blk, lambda b: (b, 0)),
                  pl.BlockSpec(blk, lambda b: (b, 1)),
                  pl.BlockSpec(blk, lambda b: (b, 2))],
        out_specs=pl.BlockSpec(blk, lambda b: (b, 0)),
        out_shape=jax.ShapeDtypeStruct((batch * ctx_len, NA_WIDTH), BF16),
        compiler_params=_params("parallel"),
        name="context_attention",
    )(qkv, qkv, qkv)


HALO_ROWS = 16


def _merge_mlp_kernel(x_ref, mod_ref, f_ref, cv_ref, cvp_ref, cvn_ref, at_ref, gt_ref, cw_ref,
                      wf_ref, wc_ref, wa_ref, wo_ref, g2_ref, w1_ref, w2_ref, fg_ref, o_ref,
                      cv_scr, x1_scr, *, seq, final_norm):
    i = pl.program_id(0)
    tm = x_ref.shape[0]
    cw = CONV_WIDTH
    u = cv_ref[:, 0:cw].astype(F32)
    gb = cv_ref[:, cw:2 * cw].astype(F32)
    gc = cv_ref[:, 2 * cw:3 * cw].astype(F32)
    z = gc * u
    zp_row = (cvp_ref[HALO_ROWS - 1:HALO_ROWS, 2 * cw:3 * cw].astype(F32)
              * cvp_ref[HALO_ROWS - 1:HALO_ROWS, 0:cw].astype(F32))
    zn_row = cvn_ref[0:1, 2 * cw:3 * cw].astype(F32) * cvn_ref[0:1, 0:cw].astype(F32)
    row = lax.broadcasted_iota(jnp.int32, (tm, cw), 0)
    if seq % tm == 0:
        tiles_per_seq = seq // tm
        zp_row = jnp.where((i % tiles_per_seq) != 0, zp_row, 0.0)
        zn_row = jnp.where((i % tiles_per_seq) != tiles_per_seq - 1, zn_row, 0.0)
    z_prev = jnp.where(row == 0, zp_row, pltpu.roll(z, 1, 0))
    z_next = jnp.where(row == tm - 1, zn_row, pltpu.roll(z, tm - 1, 0))
    if seq % tm != 0:
        pos = (i * tm + row) % seq
        z_prev = jnp.where(pos == 0, 0.0, z_prev)
        z_next = jnp.where(pos == seq - 1, 0.0, z_next)
    y = cw_ref[0:1, :] * z_prev + cw_ref[1:2, :] * z + cw_ref[2:3, :] * z_next
    cv_scr[...] = (gb * y).astype(BF16)

    d = D_MODEL
    gate = mod_ref[0, 2:3, :]

    def branches(c):
        r = slice(c * MERGE_ROW_CHUNK, (c + 1) * MERGE_ROW_CHUNK)
        return (_dot(f_ref[r, :].astype(BF16), wf_ref[...]), _dot(cv_scr[r, :], wc_ref[...]),
                _dot(at_ref[r, :], wa_ref[...]))

    nxt = branches(0)
    n_chunks = tm // MERGE_ROW_CHUNK
    for c in range(n_chunks):
        r = slice(c * MERGE_ROW_CHUNK, (c + 1) * MERGE_ROW_CHUNK)
        pf, pc, pa = nxt
        if c + 1 < n_chunks:
            nxt = branches(c + 1)
        mix = _twice_sigmoid(gt_ref[r, 0:d]) * pf
        mix = mix + _twice_sigmoid(gt_ref[r, d:2 * d]) * pc
        mix = mix + _twice_sigmoid(gt_ref[r, 2 * d:3 * d]) * pa
        proj = _dot(mix.astype(BF16), wo_ref[...])
        x1_scr[r, :] = x_ref[r, :] + (0.5 * gate) * proj

    x1 = x1_scr[...]
    h = _norm_modulate(x1, g2_ref[...], mod_ref[0, 3:4, :], mod_ref[0, 4:5, :]).astype(BF16)
    acc = None
    for c0 in range(0, MLP_HIDDEN, MLP_HIDDEN_CHUNK):
        a = jnp.maximum(_dot(h, w1_ref[:, c0:c0 + MLP_HIDDEN_CHUNK]), 0.0)
        part = _dot((a * a).astype(BF16), w2_ref[c0:c0 + MLP_HIDDEN_CHUNK, :])
        acc = part if acc is None else acc + part
    y = x1 + mod_ref[0, 5:6, :] * acc
    if final_norm:
        ms = jnp.mean(y * y, axis=-1, keepdims=True)
        y = y * lax.rsqrt(ms + EPS) * fg_ref[...]
    o_ref[...] = y


def _merge_mlp(x2d, mod, f, conv, attn, gates, conv_w, w_f, w_c, w_a, w_o, g2, w1, w2, final_g,
               layer, seq, tm, final_norm):
    m, d = x2d.shape
    assert m % seq == 0 and m % tm == 0 and (seq % tm == 0 or tm % seq == 0)
    tiles_per_seq = _tiles_per_mod_row(m, mod, tm)
    halo_per_tile = tm // HALO_ROWS
    n_halo = m // HALO_ROWS
    kern = functools.partial(_merge_mlp_kernel, seq=seq, final_norm=final_norm)
    const = lambda i: (0, 0)
    return pl.pallas_call(
        kern,
        grid=(m // tm,),
        in_specs=[
            pl.BlockSpec((tm, d), lambda i: (i, 0)),
            pl.BlockSpec((1, N_MOD, d), lambda i: (i // tiles_per_seq, 0, 0)),
            pl.BlockSpec((tm, FOURIER_WIDTH), lambda i: (i, 0)),
            pl.BlockSpec((tm, 3 * CONV_WIDTH), lambda i: (i, 0)),
            pl.BlockSpec((HALO_ROWS, 3 * CONV_WIDTH),
                         lambda i: (jnp.maximum(i * halo_per_tile - 1, 0), 0)),
            pl.BlockSpec((HALO_ROWS, 3 * CONV_WIDTH),
                         lambda i: (jnp.minimum((i + 1) * halo_per_tile, n_halo - 1), 0)),
            pl.BlockSpec((tm, NA_WIDTH), lambda i: (i, 0)),
            pl.BlockSpec((tm, 3 * d), lambda i: (i, 0)),
            pl.BlockSpec((CONV_K, CONV_WIDTH), const),
            _resident_weight_spec(w_f, layer),
            _resident_weight_spec(w_c, layer),
            _resident_weight_spec(w_a, layer),
            _resident_weight_spec(w_o, layer),
            pl.BlockSpec((1, d), const),
            _resident_weight_spec(w1, layer),
            _resident_weight_spec(w2, layer),
            pl.BlockSpec((1, d), const),
        ],
        out_specs=pl.BlockSpec((tm, d), lambda i: (i, 0)),
        out_shape=jax.ShapeDtypeStruct((m, d), F32),
        scratch_shapes=[pltpu.VMEM((tm, CONV_WIDTH), BF16), pltpu.VMEM((tm, d), F32)],
        compiler_params=_params("parallel"),
        name="merge_mlp",
    )(x2d, mod, f, conv, conv, conv, attn, gates, conv_w, w_f, w_c, w_a, w_o, g2, w1, w2, final_g)


ALL_SEGMENTS = (SEG_FOURIER, SEG_CONV, SEG_QKV, SEG_GATES)
SEGMENT_DTYPES = (F32, BF16, BF16, BF16)


def kernel(x, c, ctx, c_ctx, ada_w, ada_b, norm1_g, norm2_g, w_in, conv_w, rel_bias,
           w_fourier, w_conv, w_attn, w_o, mlp_w1, mlp_w2, final_g):
    batch, seq, d = x.shape
    ctx_len = ctx.shape[1]
    rows = seq // GRID_W
    depth = ada_w.shape[0]

    n_cond = batch + 1
    pad = (-n_cond) % 8
    cond = jnp.concatenate([c, c_ctx[None, :], jnp.zeros((pad, d), F32)], axis=0)
    mod_all = _modulation(cond, ada_w, ada_b)

    x2 = x.reshape(batch * seq, d)
    c2 = ctx.reshape(batch * ctx_len, d)
    fg = final_g.reshape(1, d)
    bias_tables = _attn_bias_table(rel_bias, rows)
    w_in_b, w_f_b, w_c_b, w_a_b, w_o_b, w1_b, w2_b = (
        w.astype(BF16) for w in (w_in, w_fourier, w_conv, w_attn, w_o, mlp_w1, mlp_w2))

    for l in range(depth):
        last = l == depth - 1
        mod_x = mod_all[l, :batch].reshape(batch, N_MOD, d)
        mod_c = mod_all[l, batch].reshape(1, N_MOD, d)
        g1 = norm1_g[l].reshape(1, d)
        g2 = norm2_g[l].reshape(1, d)

        if last:
            (kv_c,) = _in_projection(c2, mod_c, g1, w_in_b[l:l + 1, :, KV_START:KV_END], 0,
                                     ((0, KV_END - KV_START),), (BF16,), batch * ctx_len, 0, 1)
            ctx_k_col = 0
        else:
            uf_c, conv_c, qkv_c, gates_c = _in_projection(
                c2, mod_c, g1, w_in_b, l, ALL_SEGMENTS, SEGMENT_DTYPES, batch * ctx_len, 0, 1)
            kv_c, ctx_k_col = qkv_c, 1

        uf_x, conv_x, qkv_x, gates_x = _in_projection(
            x2, mod_x, g1, w_in_b, l, ALL_SEGMENTS, SEGMENT_DTYPES, TM_INPROJ, 0, 1)
        f_x = _fourier_latent(uf_x, batch, seq)
        at_x = _neighbourhood_attention(qkv_x, kv_c, ctx_k_col, bias_tables, l, batch, seq, ctx_len)
        x2 = _merge_mlp(x2, mod_x, f_x, conv_x, at_x, gates_x, conv_w[l], w_f_b, w_c_b, w_a_b, w_o_b,
                        g2, w1_b, w2_b, fg, l, seq, TM_LATENT, final_norm=last)
        if not last:
            f_c = _fourier_context(uf_c, batch, ctx_len)
            at_c = _context_attention(qkv_c, batch, ctx_len)
            c2 = _merge_mlp(c2, mod_c, f_c, conv_c, at_c, gates_c, conv_w[l], w_f_b, w_c_b, w_a_b, w_o_b,
                            g2, w1_b, w2_b, fg, l, ctx_len, TM_LATENT, final_norm=False)

    return x2.reshape(batch, seq, d)
```

```python
import functools
import math

import numpy as np
import jax
import jax.numpy as jnp
from jax import lax
from jax.experimental import pallas as pl
from jax.experimental.pallas import tpu as pltpu

D_MODEL = 1024
DEPTH = 2
GRID_W = 64
HEAD_DIM = 64
N_NA_HEADS = 8
NA_WIDTH = N_NA_HEADS * HEAD_DIM
N_FOURIER_GROUPS = 4
FOURIER_WIDTH = D_MODEL // 4
FOURIER_GROUP = FOURIER_WIDTH // N_FOURIER_GROUPS
CONV_WIDTH = D_MODEL // 4
CONV_K = 3
WIN_ROWS = 8
WIN_COLS = 16
MLP_HIDDEN = 4 * D_MODEL
N_MOD = 6
EPS = 1e-6
NEG = -1e30
LOG2E = math.log2(math.e)

SEG_FOURIER = (0, FOURIER_WIDTH)
SEG_CONV = (SEG_FOURIER[1], SEG_FOURIER[1] + 3 * CONV_WIDTH)
SEG_QKV = (SEG_CONV[1], SEG_CONV[1] + 3 * NA_WIDTH)
SEG_GATES = (SEG_QKV[1], SEG_QKV[1] + 3 * D_MODEL)
KV_START = SEG_QKV[0] + NA_WIDTH
KV_END = SEG_QKV[1]

LANES = 128
VMEM_LIMIT_BYTES = 56 * 1024 * 1024

TM_LATENT = 512
TM_INPROJ = 1024
ROWS_PER_ATTN_STEP = 4
ATTN_Q = ROWS_PER_ATTN_STEP * GRID_W
ATTN_KEY_BLOCKS = 3
ATTN_LOGITS_AHEAD = 1
ATTN_BLOCKS_PER_STEP = 2
DFT_N1 = 64
DFT_ROWS_PER_GROUP = 8
DFT_STAGE2_GROUPS = 2
MOD_COL_TILE = 1536
PROJ_COL_CHUNK = 512
MLP_HIDDEN_CHUNK = 1024
MERGE_ROW_CHUNK = 256

BF16 = jnp.bfloat16
F32 = jnp.float32


def _params(*semantics):
    return pltpu.CompilerParams(dimension_semantics=semantics, vmem_limit_bytes=VMEM_LIMIT_BYTES)


def _dot(a, b):
    return jnp.dot(a, b, preferred_element_type=F32)


def _dot_nt(a, b):
    return lax.dot_general(a, b, (((1,), (1,)), ((), ())), preferred_element_type=F32)


def _channel_dft_tables():
    c = np.arange(FOURIER_GROUP)
    ang = 2.0 * np.pi * ((c[:, None] * c[None, :]) % FOURIER_GROUP) / FOURIER_GROUP
    eye = np.eye(N_FOURIER_GROUPS)
    s = 1.0 / math.sqrt(FOURIER_GROUP)
    return np.stack([np.kron(eye, np.cos(ang) * s), np.kron(eye, np.sin(ang) * s)]).astype(np.float32)


def _latent_dft_tables(n):
    n1, n2 = DFT_N1, n // DFT_N1
    a = np.arange(n1)
    ang1 = 2.0 * np.pi * ((a[:, None] * a[None, :]) % n1) / n1
    s1 = 1.0 / math.sqrt(n1)
    f1 = np.concatenate([np.cos(ang1) * s1, -np.sin(ang1) * s1], axis=0)
    k1 = np.arange(n1)[:, None, None]
    k2 = np.arange(n2)[None, :, None]
    m2 = np.arange(n2)[None, None, :]
    ang = 2.0 * np.pi * ((m2 * (k1 + n1 * k2)) % n) / n
    s2 = 1.0 / math.sqrt(n2)
    tr, ti = np.cos(ang) * s2, -np.sin(ang) * s2
    g = np.concatenate([np.concatenate([tr, -ti], axis=2),
                        np.concatenate([ti, tr], axis=2)], axis=1)
    return f1.astype(np.float32), g.astype(np.float32)


def _context_dft_table(n):
    a = np.arange(n)
    ang = 2.0 * np.pi * ((a[:, None] * a[None, :]) % n) / n
    s = 1.0 / math.sqrt(n)
    return np.concatenate([np.cos(ang) * s, -np.sin(ang) * s], axis=0).astype(np.float32)


def _attn_key_block_start(rb, n_row_blocks):
    lead_blocks = (WIN_ROWS // 2) // ROWS_PER_ATTN_STEP
    return jnp.clip(rb - lead_blocks, 0, n_row_blocks - ATTN_KEY_BLOCKS)


def _attn_bias_table(rel_bias, rows):
    depth, h, ndr, _ = rel_bias.shape
    w = GRID_W
    rq, nkr = ROWS_PER_ATTN_STEP, ROWS_PER_ATTN_STEP * ATTN_KEY_BLOCKS
    ends = w - WIN_COLS
    ext = jnp.concatenate([
        jnp.broadcast_to(rel_bias[..., :1], (depth, h, ndr, ends)), rel_bias,
        jnp.broadcast_to(rel_bias[..., -1:], (depth, h, ndr, ends)), jnp.zeros((depth, h, ndr, 1), F32)], axis=-1)
    return pl.pallas_call(
        functools.partial(_bias_table_kernel, rows=rows),
        grid=(depth, h),
        in_specs=[pl.BlockSpec((None, None, ndr, 2 * w), lambda l, hh: (l, hh, 0, 0))],
        out_specs=pl.BlockSpec((None, 3, None, rq * w, nkr * w), lambda l, hh: (l, 0, hh, 0, 0)),
        out_shape=jax.ShapeDtypeStruct((depth, 3, h, rq * w, nkr * w), F32),
        compiler_params=_params("parallel", "parallel"),
        name="attn_bias_table",
    )(ext)


def _bias_table_kernel(ext_ref, o_ref, *, rows):
    w = GRID_W
    rq, nkr = ROWS_PER_ATTN_STEP, ROWS_PER_ATTN_STEP * ATTN_KEY_BLOCKS
    q = lax.broadcasted_iota(jnp.int32, (w, 2 * w), 0)
    t = lax.broadcasted_iota(jnp.int32, (w, 2 * w), 1)
    first_col = jnp.clip(q - WIN_COLS // 2, 0, w - WIN_COLS)
    neg = jnp.full((w, 2 * w), NEG, F32)
    left, right = [], []
    for d in range(2 * WIN_ROWS - 1):
        row = jnp.broadcast_to(ext_ref[d:d + 1, :] * LOG2E, (w, 2 * w))
        lo = pltpu.roll(row, w + 1, 1, stride=1, stride_axis=0)
        hi = pltpu.roll(row, 1, 1, stride=1, stride_axis=0)
        left.append(jnp.where((t >= first_col) & (t < first_col + WIN_COLS), lo, neg))
        right.append(jnp.where((t - w >= first_col) & (t - w < first_col + WIN_COLS), hi, neg))

    variants = [(0, 0), (rq, 0), (rows - rq, rows - nkr)]
    for v, (r0, ks) in enumerate(variants):
        for qi in range(rq):
            qr = r0 + qi
            rs = min(max(qr - WIN_ROWS // 2, 0), rows - WIN_ROWS)
            lead = rs - ks
            d0 = rs - qr + WIN_ROWS - 1

            def slab(kj, side):
                return side[d0 + kj - lead] if lead <= kj < lead + WIN_ROWS else neg

            for p in range(nkr // 2):
                tile = jnp.where(t < w, slab(2 * p, left), slab(2 * p + 1, right))
                o_ref[v, qi * w:(qi + 1) * w, p * 2 * w:(p + 1) * 2 * w] = tile


def _mod_kernel(c_ref, w_ref, b_ref, o_ref):
    c = c_ref[...]
    s = c * (1.0 / (1.0 + jnp.exp(-c)))
    o_ref[...] = jnp.dot(s, w_ref[...], precision=lax.Precision.HIGHEST,
                         preferred_element_type=F32) + b_ref[...]


def _modulation(cond, ada_w, ada_b):
    r, d = cond.shape
    depth, _, n = ada_w.shape
    tn = MOD_COL_TILE
    return pl.pallas_call(
        _mod_kernel,
        grid=(depth, n // tn),
        in_specs=[
            pl.BlockSpec((r, d), lambda l, j: (0, 0)),
            pl.BlockSpec((None, d, tn), lambda l, j: (l, 0, j)),
            pl.BlockSpec((None, 1, tn), lambda l, j: (l, 0, j)),
        ],
        out_specs=pl.BlockSpec((None, r, tn), lambda l, j: (l, 0, j)),
        out_shape=jax.ShapeDtypeStruct((depth, r, n), F32),
        compiler_params=_params("parallel", "parallel"),
        name="adaln_modulation",
    )(cond, ada_w, ada_b.reshape(depth, 1, n))


def _norm_modulate(x, g, shift, scale):
    ms = jnp.mean(x * x, axis=-1, keepdims=True)
    return (x * lax.rsqrt(ms + EPS)) * (g * (1.0 + scale)) + shift


def _twice_sigmoid(g_bf16):
    return jnp.tanh((g_bf16 * 0.5).astype(F32)) + 1.0


def _inproj_kernel(x_ref, mod_ref, g_ref, w_ref, *out_refs, segments, shift_row, scale_row):
    h = _norm_modulate(x_ref[...], g_ref[...],
                       mod_ref[0, shift_row:shift_row + 1, :], mod_ref[0, scale_row:scale_row + 1, :])
    hb = h.astype(BF16)
    for o_ref, (lo, hi) in zip(out_refs, segments):
        for c0 in range(lo, hi, PROJ_COL_CHUNK):
            c1 = min(c0 + PROJ_COL_CHUNK, hi)
            o_ref[:, c0 - lo:c1 - lo] = _dot(hb, w_ref[:, c0:c1]).astype(o_ref.dtype)


def _resident_weight_spec(w, layer):
    return pl.BlockSpec((None,) + w.shape[1:], lambda *_: (layer, 0, 0), pipeline_mode=pl.Buffered(1))


def _in_projection(x2d, mod, g, w_bf16, layer, segments, seg_dtypes, seq, tm, shift_row, scale_row):
    m, d = x2d.shape
    tiles_per_seq = seq // tm
    kern = functools.partial(_inproj_kernel, segments=tuple(segments),
                             shift_row=shift_row, scale_row=scale_row)
    return pl.pallas_call(
        kern,
        grid=(m // tm,),
        in_specs=[
            pl.BlockSpec((tm, d), lambda i: (i, 0)),
            pl.BlockSpec((1, N_MOD, d), lambda i: (i // tiles_per_seq, 0, 0)),
            pl.BlockSpec((1, d), lambda i: (0, 0)),
            _resident_weight_spec(w_bf16, layer),
        ],
        out_specs=[pl.BlockSpec((tm, hi - lo), lambda i: (i, 0)) for lo, hi in segments],
        out_shape=[jax.ShapeDtypeStruct((m, hi - lo), dt) for (lo, hi), dt in zip(segments, seg_dtypes)],
        compiler_params=_params("parallel"),
        name="norm_inproj",
    )(x2d, mod, g, w_bf16)


def _dft_stage1_kernel(x_ref, f_ref, o_ref):
    n1, rows, c = x_ref.shape
    t = DFT_ROWS_PER_GROUP
    f = f_ref[...].astype(BF16)
    for g in range(rows // t):
        x = x_ref[:, g * t:(g + 1) * t, :].reshape(n1 * t, c).astype(BF16)
        o_ref[:, :, g * t:(g + 1) * t, :] = _dot(f, x).reshape(2, n1, t, c)


def _dft_fused_kernel(x_ref, f_ref, g_ref, cs_ref, o_ref, z_scr, *, n2):
    n1, _, c = x_ref.shape
    _, r, t, _ = f_ref.shape
    f = f_ref[...].reshape(2 * r * t, n1 * t).astype(BF16)
    for g in range(n2 // t):
        x = x_ref[:, g * t:(g + 1) * t, :].reshape(n1 * t, c).astype(BF16)
        z_scr[:, :, g * t:(g + 1) * t, :] = _dot(f, x).reshape(2, r, t, c)

    cos_t, sin_t = cs_ref[0].astype(BF16), cs_ref[1].astype(BF16)
    for j in range(r):
        zz = z_scr[:, j].reshape(2 * n2, c).astype(BF16)
        p = _dot(g_ref[j].astype(BF16), zz).astype(BF16)
        o_ref[:, j, :] = _dot(p[:n2], cos_t) + _dot(p[n2:], sin_t)


def _fourier_latent(u, batch, seq):
    c = u.shape[1]
    n1, n2 = DFT_N1, seq // DFT_N1
    f1_np, g_np = _latent_dft_tables(seq)
    t = DFT_ROWS_PER_GROUP
    r2 = t * DFT_STAGE2_GROUPS
    f1 = jnp.asarray(np.kron(f1_np, np.eye(t, dtype=np.float32)))
    g = jnp.asarray(g_np)
    cs = jnp.asarray(_channel_dft_tables())
    y = pl.pallas_call(
        functools.partial(_dft_fused_kernel, n2=n2),
        grid=(batch, n1 // r2),
        in_specs=[
            pl.BlockSpec((None, n1, n2, c), lambda b, k: (b, 0, 0, 0)),
            pl.BlockSpec((2, r2, t, n1 * t), lambda b, k: (0, k, 0, 0)),
            pl.BlockSpec((r2, 2 * n2, 2 * n2), lambda b, k: (k, 0, 0)),
            pl.BlockSpec((2, c, c), lambda b, k: (0, 0, 0)),
        ],
        out_specs=pl.BlockSpec((None, n2, r2, c), lambda b, k: (b, 0, k, 0)),
        out_shape=jax.ShapeDtypeStruct((batch, n2, n1, c), F32),
        scratch_shapes=[pltpu.VMEM((2, r2, n2, c), F32)],
        compiler_params=_params("parallel", "parallel"),
        name="dft_fused",
    )(u.reshape(batch, n1, n2, c), f1.reshape(2, n1, t, n1 * t), g, cs)
    return y.reshape(batch * seq, c)


def _dft_context_kernel(u_ref, f_ref, cs_ref, o_ref, *, n):
    p = _dot(f_ref[...].astype(BF16), u_ref[...].astype(BF16)).astype(BF16)
    y = _dot(p[:n], cs_ref[0].astype(BF16)) + _dot(p[n:], cs_ref[1].astype(BF16))
    o_ref[...] = y.astype(o_ref.dtype)


def _fourier_context(u, batch, seq):
    c = u.shape[1]
    f = jnp.asarray(_context_dft_table(seq))
    cs = jnp.asarray(_channel_dft_tables())
    return pl.pallas_call(
        functools.partial(_dft_context_kernel, n=seq),
        grid=(batch,),
        in_specs=[
            pl.BlockSpec((seq, c), lambda b: (b, 0)),
            pl.BlockSpec((2 * seq, seq), lambda b: (0, 0)),
            pl.BlockSpec((2, c, c), lambda b: (0, 0, 0)),
        ],
        out_specs=pl.BlockSpec((seq, c), lambda b: (b, 0)),
        out_shape=jax.ShapeDtypeStruct((batch * seq, c), BF16),
        compiler_params=_params("parallel"),
        name="dft_context",
    )(u, f, cs)


def _attend_heads(problems):
    q_scale = HEAD_DIM ** -0.5 * LOG2E

    def lanes(h):
        return slice((h // 2) * LANES, (h // 2 + 1) * LANES)

    def own_lanes(shape, h):
        low = lax.broadcasted_iota(jnp.int32, shape, 1) < HEAD_DIM
        return low if h % 2 == 0 else jnp.logical_not(low)

    def logits(unit):
        h, g = unit
        q_ref, key_refs, _, bias_fn, _ = problems[g]
        q_pair = q_ref[:, lanes(h)]
        qs = (q_pair.astype(F32) * q_scale).astype(BF16)
        qm = jnp.where(own_lanes(qs.shape, h), qs, jnp.zeros_like(qs))
        out = []
        for j, k_ref in enumerate(key_refs):
            sj = _dot_nt(qm, k_ref[:, lanes(h)])
            bj = bias_fn(h, j)
            out.append(sj if bj is None else sj + bj)
        return out

    units = [(h, g) for h in range(N_NA_HEADS) for g in range(len(problems))]
    ahead = ATTN_LOGITS_AHEAD
    queue = [logits(u) for u in units[:ahead]]
    even_out = [None] * len(problems)
    for n, (h, g) in enumerate(units):
        s = queue.pop(0)
        if n + ahead < len(units):
            queue.append(logits(units[n + ahead]))
        _, _, value_refs, _, o_ref = problems[g]
        m = s[0]
        for sj in s[1:]:
            m = jnp.maximum(m, sj)
        m = jnp.max(m, axis=-1, keepdims=True)
        p = jnp.concatenate([jnp.exp2(sj - m).astype(BF16) for sj in s], axis=1)
        v = jnp.concatenate([v_ref[:, lanes(h)] for v_ref in value_refs], axis=0)
        v_aug = jnp.where(own_lanes(v.shape, h), v, jnp.ones_like(v))
        acc = _dot(p, v_aug)
        denom = pltpu.roll(acc, HEAD_DIM, 1)
        out = acc * (1.0 / denom)
        if h % 2 == 0:
            even_out[g] = out
        else:
            o_ref[:, lanes(h)] = jnp.where(own_lanes(out.shape, 0), even_out[g], out).astype(o_ref.dtype)


def _natten_kernel(q_ref, *refs):
    nb, ng, tq = ATTN_KEY_BLOCKS, ATTN_BLOCKS_PER_STEP, ATTN_Q
    kv_refs, (kc_ref, vc_ref), bias_refs, o_ref = (
        refs[:2 * nb * ng], refs[2 * nb * ng:2 * nb * ng + 2], refs[2 * nb * ng + 2:-1], refs[-1])

    def problem(g):
        k_refs = kv_refs[2 * nb * g:2 * nb * g + nb]
        v_refs = kv_refs[2 * nb * g + nb:2 * nb * (g + 1)]
        bias_ref = bias_refs[g]

        def bias_fn(h, j):
            return bias_ref[h, :, j * tq:(j + 1) * tq] if j < nb else None

        rows = pl.ds(g * tq, tq)
        return (q_ref.at[rows], list(k_refs) + [kc_ref], list(v_refs) + [vc_ref], bias_fn, o_ref.at[rows])

    _attend_heads([problem(g) for g in range(ng)])


def _neighbourhood_attention(qkv, kv_ctx, ctx_k_col, bias_tables, layer, batch, seq, ctx_len):
    m = qkv.shape[0]
    tq, ng = ATTN_Q, ATTN_BLOCKS_PER_STEP
    nrb = seq // tq
    assert ctx_len == tq, "context keys are processed as one key block"

    def q_map(b, r):
        return (b * (nrb // ng) + r, 0)

    def kv_map(g, j, col):
        return lambda b, r: (b * nrb + _attn_key_block_start(ng * r + g, nrb) + j, col)

    def bias_map(g):
        def index(b, r):
            rb = ng * r + g
            return (layer, jnp.where(rb == 0, 0, jnp.where(rb == nrb - 1, 2, 1)), 0, 0, 0)
        return index

    blk = (tq, NA_WIDTH)
    in_specs = [pl.BlockSpec((ng * tq, NA_WIDTH), q_map)]
    for g in range(ng):
        in_specs += [pl.BlockSpec(blk, kv_map(g, j, 1)) for j in range(ATTN_KEY_BLOCKS)]
        in_specs += [pl.BlockSpec(blk, kv_map(g, j, 2)) for j in range(ATTN_KEY_BLOCKS)]
    in_specs += [
        pl.BlockSpec((ctx_len, NA_WIDTH), lambda b, r: (b, ctx_k_col)),
        pl.BlockSpec((ctx_len, NA_WIDTH), lambda b, r: (b, ctx_k_col + 1)),
    ]
    in_specs += [pl.BlockSpec((None, None, N_NA_HEADS, tq, ATTN_KEY_BLOCKS * tq), bias_map(g))
                 for g in range(ng)]
    args = [qkv] * (1 + 2 * ATTN_KEY_BLOCKS * ng) + [kv_ctx, kv_ctx] + [bias_tables] * ng
    return pl.pallas_call(
        _natten_kernel,
        grid=(batch, nrb // ng),
        in_specs=in_specs,
        out_specs=pl.BlockSpec((ng * tq, NA_WIDTH), q_map),
        out_shape=jax.ShapeDtypeStruct((m, NA_WIDTH), BF16),
        compiler_params=_params("parallel", "arbitrary"),
        name="neighbourhood_attention",
    )(*args)


def _ctx_attn_kernel(q_ref, k_ref, v_ref, o_ref):
    _attend_heads([(q_ref, [k_ref], [v_ref], lambda h, j: None, o_ref)])


def _context_attention(qkv, batch, ctx_len):
    blk = (ctx_len, NA_WIDTH)
    return pl.pallas_call(
        _ctx_attn_kernel,
        grid=(batch,),
        in_specs=[pl.BlockSpec(blk, lambda b: (b, 0)),
                  pl.BlockSpec(blk, lambda b: (b, 1)),
                  pl.BlockSpec(blk, lambda b: (b, 2))],
        out_specs=pl.BlockSpec(blk, lambda b: (b, 0)),
        out_shape=jax.ShapeDtypeStruct((batch * ctx_len, NA_WIDTH), BF16),
        compiler_params=_params("parallel"),
        name="context_attention",
    )(qkv, qkv, qkv)


HALO_ROWS = 16


def _merge_mlp_kernel(x_ref, mod_ref, f_ref, cv_ref, cvp_ref, cvn_ref, at_ref, gt_ref, cw_ref,
                      wf_ref, wc_ref, wa_ref, wo_ref, g2_ref, w1_ref, w2_ref, fg_ref, o_ref,
                      cv_scr, x1_scr, *, tiles_per_seq, final_norm):
    i = pl.program_id(0)
    tm = x_ref.shape[0]
    cw = CONV_WIDTH
    has_prev = (i % tiles_per_seq) != 0
    has_next = (i % tiles_per_seq) != tiles_per_seq - 1

    u = cv_ref[:, 0:cw].astype(F32)
    gb = cv_ref[:, cw:2 * cw].astype(F32)
    gc = cv_ref[:, 2 * cw:3 * cw].astype(F32)
    z = gc * u
    zp_row = (cvp_ref[HALO_ROWS - 1:HALO_ROWS, 2 * cw:3 * cw].astype(F32)
              * cvp_ref[HALO_ROWS - 1:HALO_ROWS, 0:cw].astype(F32))
    zn_row = cvn_ref[0:1, 2 * cw:3 * cw].astype(F32) * cvn_ref[0:1, 0:cw].astype(F32)
    zp_row = jnp.where(has_prev, zp_row, 0.0)
    zn_row = jnp.where(has_next, zn_row, 0.0)
    row = lax.broadcasted_iota(jnp.int32, (tm, cw), 0)
    z_prev = jnp.where(row == 0, zp_row, pltpu.roll(z, 1, 0))
    z_next = jnp.where(row == tm - 1, zn_row, pltpu.roll(z, tm - 1, 0))
    y = cw_ref[0:1, :] * z_prev + cw_ref[1:2, :] * z + cw_ref[2:3, :] * z_next
    cv_scr[...] = (gb * y).astype(BF16)

    d = D_MODEL
    gate = mod_ref[0, 2:3, :]

    def branches(c):
        r = slice(c * MERGE_ROW_CHUNK, (c + 1) * MERGE_ROW_CHUNK)
        return (_dot(f_ref[r, :].astype(BF16), wf_ref[...]), _dot(cv_scr[r, :], wc_ref[...]),
                _dot(at_ref[r, :], wa_ref[...]))

    nxt = branches(0)
    n_chunks = tm // MERGE_ROW_CHUNK
    for c in range(n_chunks):
        r = slice(c * MERGE_ROW_CHUNK, (c + 1) * MERGE_ROW_CHUNK)
        pf, pc, pa = nxt
        if c + 1 < n_chunks:
            nxt = branches(c + 1)
        mix = _twice_sigmoid(gt_ref[r, 0:d]) * pf
        mix = mix + _twice_sigmoid(gt_ref[r, d:2 * d]) * pc
        mix = mix + _twice_sigmoid(gt_ref[r, 2 * d:3 * d]) * pa
        proj = _dot(mix.astype(BF16), wo_ref[...])
        x1_scr[r, :] = x_ref[r, :] + (0.5 * gate) * proj

    x1 = x1_scr[...]
    h = _norm_modulate(x1, g2_ref[...], mod_ref[0, 3:4, :], mod_ref[0, 4:5, :]).astype(BF16)
    acc = None
    for c0 in range(0, MLP_HIDDEN, MLP_HIDDEN_CHUNK):
        a = jnp.maximum(_dot(h, w1_ref[:, c0:c0 + MLP_HIDDEN_CHUNK]), 0.0)
        part = _dot((a * a).astype(BF16), w2_ref[c0:c0 + MLP_HIDDEN_CHUNK, :])
        acc = part if acc is None else acc + part
    y = x1 + mod_ref[0, 5:6, :] * acc
    if final_norm:
        ms = jnp.mean(y * y, axis=-1, keepdims=True)
        y = y * lax.rsqrt(ms + EPS) * fg_ref[...]
    o_ref[...] = y


def _merge_mlp(x2d, mod, f, conv, attn, gates, conv_w, w_f, w_c, w_a, w_o, g2, w1, w2, final_g,
               layer, seq, tm, final_norm):
    m, d = x2d.shape
    tiles_per_seq = seq // tm
    halo_per_tile = tm // HALO_ROWS
    n_halo = m // HALO_ROWS
    kern = functools.partial(_merge_mlp_kernel, tiles_per_seq=tiles_per_seq, final_norm=final_norm)
    const = lambda i: (0, 0)
    return pl.pallas_call(
        kern,
        grid=(m // tm,),
        in_specs=[
            pl.BlockSpec((tm, d), lambda i: (i, 0)),
            pl.BlockSpec((1, N_MOD, d), lambda i: (i // tiles_per_seq, 0, 0)),
            pl.BlockSpec((tm, FOURIER_WIDTH), lambda i: (i, 0)),
            pl.BlockSpec((tm, 3 * CONV_WIDTH), lambda i: (i, 0)),
            pl.BlockSpec((HALO_ROWS, 3 * CONV_WIDTH),
                         lambda i: (jnp.maximum(i * halo_per_tile - 1, 0), 0)),
            pl.BlockSpec((HALO_ROWS, 3 * CONV_WIDTH),
                         lambda i: (jnp.minimum((i + 1) * halo_per_tile, n_halo - 1), 0)),
            pl.BlockSpec((tm, NA_WIDTH), lambda i: (i, 0)),
            pl.BlockSpec((tm, 3 * d), lambda i: (i, 0)),
            pl.BlockSpec((CONV_K, CONV_WIDTH), const),
            _resident_weight_spec(w_f, layer),
            _resident_weight_spec(w_c, layer),
            _resident_weight_spec(w_a, layer),
            _resident_weight_spec(w_o, layer),
            pl.BlockSpec((1, d), const),
            _resident_weight_spec(w1, layer),
            _resident_weight_spec(w2, layer),
            pl.BlockSpec((1, d), const),
        ],
        out_specs=pl.BlockSpec((tm, d), lambda i: (i, 0)),
        out_shape=jax.ShapeDtypeStruct((m, d), F32),
        scratch_shapes=[pltpu.VMEM((tm, CONV_WIDTH), BF16), pltpu.VMEM((tm, d), F32)],
        compiler_params=_params("parallel"),
        name="merge_mlp",
    )(x2d, mod, f, conv, conv, conv, attn, gates, conv_w, w_f, w_c, w_a, w_o, g2, w1, w2, final_g)


ALL_SEGMENTS = (SEG_FOURIER, SEG_CONV, SEG_QKV, SEG_GATES)
SEGMENT_DTYPES = (F32, BF16, BF16, BF16)


def kernel(x, c, ctx, c_ctx, ada_w, ada_b, norm1_g, norm2_g, w_in, conv_w, rel_bias,
           w_fourier, w_conv, w_attn, w_o, mlp_w1, mlp_w2, final_g):
    batch, seq, d = x.shape
    ctx_len = ctx.shape[1]
    rows = seq // GRID_W
    depth = ada_w.shape[0]

    n_cond = batch + 1
    pad = (-n_cond) % 8
    cond = jnp.concatenate([c, c_ctx[None, :], jnp.zeros((pad, d), F32)], axis=0)
    mod_all = _modulation(cond, ada_w, ada_b)

    x2 = x.reshape(batch * seq, d)
    c2 = ctx.reshape(batch * ctx_len, d)
    fg = final_g.reshape(1, d)
    bias_tables = _attn_bias_table(rel_bias, rows)
    w_in_b, w_f_b, w_c_b, w_a_b, w_o_b, w1_b, w2_b = (
        w.astype(BF16) for w in (w_in, w_fourier, w_conv, w_attn, w_o, mlp_w1, mlp_w2))

    for l in range(depth):
        last = l == depth - 1
        mod_x = mod_all[l, :batch].reshape(batch, N_MOD, d)
        mod_c = jnp.broadcast_to(mod_all[l, batch].reshape(1, N_MOD, d), (batch, N_MOD, d))
        g1 = norm1_g[l].reshape(1, d)
        g2 = norm2_g[l].reshape(1, d)

        if last:
            (kv_c,) = _in_projection(c2, mod_c, g1, w_in_b[l:l + 1, :, KV_START:KV_END], 0,
                                     ((0, KV_END - KV_START),), (BF16,), ctx_len, ctx_len, 0, 1)
            ctx_k_col = 0
        else:
            uf_c, conv_c, qkv_c, gates_c = _in_projection(
                c2, mod_c, g1, w_in_b, l, ALL_SEGMENTS, SEGMENT_DTYPES, ctx_len, ctx_len, 0, 1)
            kv_c, ctx_k_col = qkv_c, 1

        uf_x, conv_x, qkv_x, gates_x = _in_projection(
            x2, mod_x, g1, w_in_b, l, ALL_SEGMENTS, SEGMENT_DTYPES, seq, TM_INPROJ, 0, 1)
        f_x = _fourier_latent(uf_x, batch, seq)
        at_x = _neighbourhood_attention(qkv_x, kv_c, ctx_k_col, bias_tables, l, batch, seq, ctx_len)
        x2 = _merge_mlp(x2, mod_x, f_x, conv_x, at_x, gates_x, conv_w[l], w_f_b, w_c_b, w_a_b, w_o_b,
                        g2, w1_b, w2_b, fg, l, seq, TM_LATENT, final_norm=last)
        if not last:
            f_c = _fourier_context(uf_c, batch, ctx_len)
            at_c = _context_attention(qkv_c, batch, ctx_len)
            c2 = _merge_mlp(c2, mod_c, f_c, conv_c, at_c, gates_c, conv_w[l], w_f_b, w_c_b, w_a_b, w_o_b,
                            g2, w1_b, w2_b, fg, l, ctx_len, ctx_len, final_norm=False)

    return x2.reshape(batch, seq, d)
```
